```python
import jax, jax.numpy as jnp
from jax import lax
import numpy as np

D_MODEL = 1024
BATCH = 4
SEQ = 8192
DEPTH = 2
DEC_BATCH = 128
DEC_SEQ = 8
PAST_LEN = 16384
PAGE_SIZE = 128

N_EVEN = (DEPTH + 1) // 2
N_ODD = DEPTH // 2
CHUNK = 128
A_GROUPS = 4
A_GD = 128
A_WIDTH = A_GROUPS * A_GD
POOL_WINDOWS = (2, 4, 8, 16)
B_GD = 128
B_WIDTH = len(POOL_WINDOWS) * B_GD
POOL_CTX = max(POOL_WINDOWS) - 1
AB_IN = 2 * A_WIDTH + B_WIDTH
AB_OUT = A_WIDTH + B_WIDTH
N_HEADS = 16
N_KV = 4
HEAD_DIM = 64
GQA = N_HEADS // N_KV
WINDOW = 128
ROT_DIM = HEAD_DIM // 4
ROPE_THETA = 500000.0
QKV_OUT = (N_HEADS + 2 * N_KV) * HEAD_DIM
N_MEM = 256
MEM_HEADS = 4
MEM_HD = 128
MEM_WIDTH = MEM_HEADS * MEM_HD
D_FF = 2816
CONV_W = 3
EPS = 1e-6

kernel_name = "hybrid_gmlp_pool_swa_mem_convffn_step"


def rmsnorm(x, g):
    xf = x.astype(jnp.float32)
    y = xf * lax.rsqrt(jnp.mean(xf * xf, -1, keepdims=True) + EPS)
    return (y * g.astype(jnp.float32)).astype(x.dtype)


def layernorm(x, g):
    xf = x.astype(jnp.float32)
    xc = xf - jnp.mean(xf, -1, keepdims=True)
    y = xc * lax.rsqrt(jnp.mean(xc * xc, -1, keepdims=True) + EPS)
    return (y * g.astype(jnp.float32)).astype(x.dtype)


def gelu(x):
    return jax.nn.gelu(x, approximate=False)


def rope_partial(x, pos):
    half = ROT_DIM // 2
    inv = ROPE_THETA ** (-jnp.arange(half, dtype=jnp.float32) / half)
    ang = pos.astype(jnp.float32)[:, None] * inv[None, :]
    cos = jnp.cos(ang)[None, :, None, :]
    sin = jnp.sin(ang)[None, :, None, :]
    xr = x[..., :ROT_DIM].astype(jnp.float32)
    x1, x2 = xr[..., :half], xr[..., half:]
    rot = jnp.concatenate([x1 * cos - x2 * sin, x2 * cos + x1 * sin], -1).astype(x.dtype)
    return jnp.concatenate([rot, x[..., ROT_DIM:]], -1)


def mix_ab(h, pool_ctx, pos, w_in, v_gain, w_s, b_s, pool_w, pool_scale, w_out):
    N, T, _ = h.shape
    proj = h @ w_in
    u = gelu(proj[..., :A_WIDTH])
    v = layernorm(gelu(proj[..., A_WIDTH:2 * A_WIDTH]), v_gain)
    p = proj[..., 2 * A_WIDTH:]
    L = min(T, CHUNK)
    nc = T // L
    ws = jnp.where(jnp.tril(jnp.ones((L, L), bool)), w_s[:, :L, :L], 0.0)
    vc = v.reshape(N, nc, L, A_GROUPS, A_GD)
    sg = jnp.einsum('gij,ncjgd->ncigd', ws, vc) + b_s[:, :L].T[None, None, :, :, None]
    a_out = u * sg.reshape(N, T, A_WIDTH)
    p_ext = jnp.concatenate([pool_ctx, p], 1)
    cs = jnp.cumsum(p_ext.astype(jnp.float32), axis=1)
    cs = jnp.concatenate([jnp.zeros((N, 1, B_WIDTH), jnp.float32), cs], 1)
    outs = []
    for gi, w in enumerate(POOL_WINDOWS):
        sl = slice(gi * B_GD, (gi + 1) * B_GD)
        s = cs[:, POOL_CTX + 1:POOL_CTX + 1 + T, sl] - cs[:, POOL_CTX + 1 - w:POOL_CTX + 1 - w + T, sl]
        cnt = jnp.minimum(pos + 1, w).astype(jnp.float32)[None, :, None]
        outs.append(s / cnt)
    pooled = (jnp.concatenate(outs, -1) - p.astype(jnp.float32)).astype(h.dtype)
    pooled = pooled.reshape(N, T, len(POOL_WINDOWS), B_GD)
    b_out = jnp.einsum('ntgd,gde->ntge', pooled, pool_w).reshape(N, T, B_WIDTH) * pool_scale
    y = jnp.concatenate([a_out, b_out], -1) @ w_out
    return y, p_ext[:, -POOL_CTX:], v


def sink_attend(q, k, v, mask, sinks):
    s = jnp.einsum('...qkgd,...skd->...kgqs', q, k).astype(jnp.float32) * (HEAD_DIM ** -0.5)
    s = jnp.where(mask, s, -jnp.inf)
    sink = jnp.broadcast_to(sinks.reshape(N_KV, GQA, 1, 1).astype(jnp.float32), s.shape[:-1] + (1,))
    pr = jax.nn.softmax(jnp.concatenate([s, sink], -1), axis=-1)[..., :-1]
    return jnp.einsum('...kgqs,...skd->...qkgd', pr.astype(v.dtype), v)


def swa_qkv(h, pos, w_qkv, q_gain, k_gain):
    N, T, _ = h.shape
    qkv = h @ w_qkv
    q = qkv[..., :N_HEADS * HEAD_DIM].reshape(N, T, N_HEADS, HEAD_DIM)
    k = qkv[..., N_HEADS * HEAD_DIM:(N_HEADS + N_KV) * HEAD_DIM].reshape(N, T, N_KV, HEAD_DIM)
    v = qkv[..., (N_HEADS + N_KV) * HEAD_DIM:].reshape(N, T, N_KV, HEAD_DIM)
    q = rope_partial(rmsnorm(q, q_gain), pos)
    k = rope_partial(rmsnorm(k, k_gain), pos)
    return q, k, v


def swa_prompt(h, pos, w_qkv, q_gain, k_gain, sinks, w_o):
    N, T, _ = h.shape
    q, k, v = swa_qkv(h, pos, w_qkv, q_gain, k_gain)
    nb = T // WINDOW
    qb = q.reshape(N, nb, WINDOW, N_KV, GQA, HEAD_DIM)
    kb = k.reshape(N, nb, WINDOW, N_KV, HEAD_DIM)
    vb = v.reshape(N, nb, WINDOW, N_KV, HEAD_DIM)
    pad = ((0, 0), (1, 0), (0, 0), (0, 0), (0, 0))
    kk = jnp.concatenate([jnp.pad(kb, pad)[:, :-1], kb], 2)
    vv = jnp.concatenate([jnp.pad(vb, pad)[:, :-1], vb], 2)
    i = jnp.arange(WINDOW)[:, None]
    s = jnp.arange(2 * WINDOW)[None, :]
    band = (s > i) & (s <= i + WINDOW)
    valid = band[None] & ((jnp.arange(nb)[:, None, None] > 0) | (s[None] >= WINDOW))
    o = sink_attend(qb, kk, vv, valid[:, None, None], sinks)
    y = o.reshape(N, T, N_HEADS * HEAD_DIM) @ w_o
    return y, k[:, -WINDOW:], v[:, -WINDOW:]


def swa_sample(h, k_ctx, v_ctx, pos, w_qkv, q_gain, k_gain, sinks, w_o):
    N, T, _ = h.shape
    q, k, v = swa_qkv(h, pos, w_qkv, q_gain, k_gain)
    kk = jnp.concatenate([k_ctx, k], 1)
    vv = jnp.concatenate([v_ctx, v], 1)
    i = jnp.arange(T)[:, None]
    s = jnp.arange(WINDOW + T)[None, :]
    mask = (s > i) & (s <= i + WINDOW)
    o = sink_attend(q.reshape(N, T, N_KV, GQA, HEAD_DIM), kk, vv, mask, sinks)
    y = o.reshape(N, T, N_HEADS * HEAD_DIM) @ w_o
    return y, kk[:, -WINDOW:], vv[:, -WINDOW:]


def mem_kv(mem, g_mem, w_kv, k_gain):
    N = mem.shape[0]
    m = rmsnorm(mem, g_mem) @ w_kv
    k = rmsnorm(m[..., :MEM_WIDTH].reshape(N, N_MEM, MEM_HEADS, MEM_HD), k_gain)
    v = m[..., MEM_WIDTH:].reshape(N, N_MEM, MEM_HEADS, MEM_HD)
    return k, v


def mem_attend(h, k, v, w_q, q_gain, w_o):
    N, T, _ = h.shape
    q = rmsnorm((h @ w_q).reshape(N, T, MEM_HEADS, MEM_HD), q_gain)
    s = jnp.einsum('nthd,nmhd->nhtm', q, k).astype(jnp.float32) * (MEM_HD ** -0.5)
    pr = jax.nn.softmax(s, axis=-1).astype(v.dtype)
    o = jnp.einsum('nhtm,nmhd->nthd', pr, v)
    return o.reshape(N, T, MEM_WIDTH) @ w_o


def conv_ffn(h, g_ctx, w_up, conv_w, conv_b, w_down):
    T = h.shape[1]
    up = h @ w_up
    g, u = up[..., :D_FF], up[..., D_FF:]
    g_ext = jnp.concatenate([g_ctx, g], 1)
    gc = conv_b
    for j in range(CONV_W):
        gc = gc + conv_w[j] * g_ext[:, j:j + T]
    y = (gelu(gc) * u) @ w_down
    return y, g_ext[:, -(CONV_W - 1):]


def setup_inputs(seed: int = 0) -> dict:
    key = jax.random.key(seed)
    ks = iter(jax.random.split(key, 48))

    def nrm(shape, scale):
        return jax.random.normal(next(ks), shape, jnp.float32) * scale

    def gain(shape):
        return 1.0 + nrm(shape, 0.02)

    D = D_MODEL
    return {
        "x_prompt": nrm((BATCH, SEQ, D), 1.0),
        "x_sample": nrm((DEC_BATCH, DEC_SEQ, D), 1.0),
        "cache_pool": nrm((N_EVEN, DEC_BATCH, POOL_CTX, B_WIDTH), 1.0),
        "cache_swa_k": nrm((N_ODD, DEC_BATCH, WINDOW, N_KV, HEAD_DIM), 1.0),
        "cache_swa_v": nrm((N_ODD, DEC_BATCH, WINDOW, N_KV, HEAD_DIM), 1.0),
        "cache_mem_k": nrm((DEPTH, DEC_BATCH, N_MEM, MEM_HEADS, MEM_HD), 1.0),
        "cache_mem_v": nrm((DEPTH, DEC_BATCH, N_MEM, MEM_HEADS, MEM_HD), 1.0),
        "cache_ffn_conv": nrm((DEPTH, DEC_BATCH, CONV_W - 1, D_FF), 1.0),
        "mem_prompt": nrm((BATCH, N_MEM, D), 1.0),
        "ln_mix": gain((DEPTH, D)),
        "ln_mem": gain((DEPTH, D)),
        "ln_memkv": gain((DEPTH, D)),
        "ln_ffn": gain((DEPTH, D)),
        "ab_w_in": nrm((N_EVEN, D, AB_IN), D ** -0.5),
        "ab_v_gain": gain((N_EVEN, A_WIDTH)),
        "ab_w_s": nrm((N_EVEN, A_GROUPS, CHUNK, CHUNK), CHUNK ** -0.5),
        "ab_b_s": 1.0 + nrm((N_EVEN, A_GROUPS, CHUNK), 0.1),
        "ab_pool_w": nrm((N_EVEN, len(POOL_WINDOWS), B_GD, B_GD), B_GD ** -0.5),
        "ab_pool_scale": 1.0 + nrm((N_EVEN, B_WIDTH), 0.1),
        "ab_w_out": nrm((N_EVEN, AB_OUT, D), AB_OUT ** -0.5),
        "c_w_qkv": nrm((N_ODD, D, QKV_OUT), D ** -0.5),
        "c_q_gain": gain((N_ODD, HEAD_DIM)),
        "c_k_gain": gain((N_ODD, HEAD_DIM)),
        "c_sinks": nrm((N_ODD, N_HEADS), 0.5),
        "c_w_o": nrm((N_ODD, N_HEADS * HEAD_DIM, D), (N_HEADS * HEAD_DIM) ** -0.5),
        "m_w_q": nrm((DEPTH, D, MEM_WIDTH), D ** -0.5),
        "m_w_kv": nrm((DEPTH, D, 2 * MEM_WIDTH), D ** -0.5),
        "m_q_gain": gain((DEPTH, MEM_HD)),
        "m_k_gain": gain((DEPTH, MEM_HD)),
        "m_w_o": nrm((DEPTH, MEM_WIDTH, D), MEM_WIDTH ** -0.5),
        "f_w_up": nrm((DEPTH, D, 2 * D_FF), D ** -0.5),
        "f_conv_w": nrm((DEPTH, CONV_W, D_FF), CONV_W ** -0.5),
        "f_conv_b": nrm((DEPTH, D_FF), 0.02),
        "f_w_down": nrm((DEPTH, D_FF, D), D_FF ** -0.5),
    }


def reference(x_prompt, x_sample, cache_pool, cache_swa_k, cache_swa_v, cache_mem_k, cache_mem_v,
              cache_ffn_conv, mem_prompt, ln_mix, ln_mem, ln_memkv, ln_ffn, ab_w_in, ab_v_gain,
              ab_w_s, ab_b_s, ab_pool_w, ab_pool_scale, ab_w_out, c_w_qkv, c_q_gain, c_k_gain,
              c_sinks, c_w_o, m_w_q, m_w_kv, m_q_gain, m_k_gain, m_w_o, f_w_up, f_conv_w,
              f_conv_b, f_w_down):
    xp, xs = x_prompt, x_sample
    Bp, Tp, _ = xp.shape
    Bs, Ts, _ = xs.shape
    pos_p = jnp.arange(Tp)
    pos_s = PAST_LEN + jnp.arange(Ts)
    pool_p, pool_s, chunk_v_s = [], [], []
    swa_kp, swa_vp, swa_ks, swa_vs = [], [], [], []
    mem_kp, mem_vp, conv_p, conv_s = [], [], [], []
    for l in range(DEPTH):
        j = l // 2
        hp = rmsnorm(xp, ln_mix[l])
        hs = rmsnorm(xs, ln_mix[l])
        if l % 2 == 0:
            zp = jnp.zeros((Bp, POOL_CTX, B_WIDTH), xp.dtype)
            yp, stp, _ = mix_ab(hp, zp, pos_p, ab_w_in[j], ab_v_gain[j], ab_w_s[j], ab_b_s[j],
                                ab_pool_w[j], ab_pool_scale[j], ab_w_out[j])
            ys, sts, vs = mix_ab(hs, cache_pool[j], pos_s, ab_w_in[j], ab_v_gain[j], ab_w_s[j],
                                 ab_b_s[j], ab_pool_w[j], ab_pool_scale[j], ab_w_out[j])
            pool_p.append(stp)
            pool_s.append(sts)
            chunk_v_s.append(vs)
        else:
            yp, kp, vp = swa_prompt(hp, pos_p, c_w_qkv[j], c_q_gain[j], c_k_gain[j], c_sinks[j], c_w_o[j])
            ys, ks_, vs_ = swa_sample(hs, cache_swa_k[j], cache_swa_v[j], pos_s, c_w_qkv[j],
                                      c_q_gain[j], c_k_gain[j], c_sinks[j], c_w_o[j])
            swa_kp.append(kp)
            swa_vp.append(vp)
            swa_ks.append(ks_)
            swa_vs.append(vs_)
        xp = xp + yp
        xs = xs + ys
        mk, mv = mem_kv(mem_prompt, ln_memkv[l], m_w_kv[l], m_k_gain[l])
        mem_kp.append(mk)
        mem_vp.append(mv)
        xp = xp + mem_attend(rmsnorm(xp, ln_mem[l]), mk, mv, m_w_q[l], m_q_gain[l], m_w_o[l])
        xs = xs + mem_attend(rmsnorm(xs, ln_mem[l]), cache_mem_k[l], cache_mem_v[l], m_w_q[l],
                             m_q_gain[l], m_w_o[l])
        zc = jnp.zeros((Bp, CONV_W - 1, D_FF), xp.dtype)
        fp, cp = conv_ffn(rmsnorm(xp, ln_ffn[l]), zc, f_w_up[l], f_conv_w[l], f_conv_b[l], f_w_down[l])
        fs, cs = conv_ffn(rmsnorm(xs, ln_ffn[l]), cache_ffn_conv[l], f_w_up[l], f_conv_w[l],
                          f_conv_b[l], f_w_down[l])
        conv_p.append(cp)
        conv_s.append(cs)
        xp = xp + fp
        xs = xs + fs
    return (xp, xs,
            jnp.stack(pool_p), jnp.stack(pool_s), jnp.stack(chunk_v_s),
            jnp.stack(swa_kp), jnp.stack(swa_vp), jnp.stack(swa_ks), jnp.stack(swa_vs),
            jnp.stack(mem_kp), jnp.stack(mem_vp),
            jnp.stack(conv_p), jnp.stack(conv_s))
```

```python
import functools
import math

import jax
import jax.numpy as jnp
import numpy as np
from jax import lax
from jax.experimental import pallas as pl
from jax.experimental.pallas import tpu as pltpu

D_MODEL = 1024
PAST_LEN = 16384
CHUNK = 128
A_GROUPS = 4
A_GD = 128
A_WIDTH = A_GROUPS * A_GD
POOL_WINDOWS = (2, 4, 8, 16)
B_GD = 128
B_WIDTH = len(POOL_WINDOWS) * B_GD
POOL_CTX = max(POOL_WINDOWS) - 1
AB_IN = 2 * A_WIDTH + B_WIDTH
N_HEADS = 16
N_KV = 4
HEAD_DIM = 64
GQA = N_HEADS // N_KV
WINDOW = 128
ROT_DIM = HEAD_DIM // 4
ROPE_THETA = 500000.0
QKV_OUT = (N_HEADS + 2 * N_KV) * HEAD_DIM
N_MEM = 256
MEM_HEADS = 4
MEM_HD = 128
MEM_WIDTH = MEM_HEADS * MEM_HD
D_FF = 2816
CONV_W = 3
EPS = 1e-6

LANES = 128
SUBLANES = 8
VMEM_LIMIT = 56 * 1024 * 1024

F32 = jnp.float32
BF16 = jnp.bfloat16

PROMPT_TILE = 512
FFN_CHUNK = 256
FFN_NCH = D_FF // FFN_CHUNK
SAMPLE_BLOCK = 64


def _gelu(x):
    return 0.5 * x * (1.0 + lax.erf(x * np.float32(math.sqrt(0.5))))


def _rms(x, g):
    return x * lax.rsqrt(jnp.mean(x * x, -1, keepdims=True) + EPS) * g


def _const_spec(shape):
    zeros = (0,) * len(shape)
    return pl.BlockSpec(shape, lambda *_: zeros, pipeline_mode=pl.Buffered(1))


def _params(sem):
    return pltpu.CompilerParams(dimension_semantics=sem, vmem_limit_bytes=VMEM_LIMIT)


def _ab_body(x_ref, g_ref, win_ref, vgain_ref, wtril_ref, bsb_ref, ws_ref, bs_ref, pwbd_ref,
             pscale_ref, wout_ref, ctx_ref, *rest, tm, pfx, shift, pos0, sample):
    if sample:
        o_ref, ptail_ref, v_ref, pbuf_ref, carry_ref, cat_ref = rest
    else:
        o_ref, ptail_ref, pbuf_ref, carry_ref, cat_ref = rest
    t = pl.program_id(1)
    nt = pl.num_programs(1)

    @pl.when(t == 0)
    def _():
        carry_ref[...] = ctx_ref[0]

    x = x_ref[0]
    h = _rms(x, g_ref[...]).astype(BF16)
    proj = jnp.dot(h, win_ref[...], preferred_element_type=F32)
    u = _gelu(proj[:, :A_WIDTH])
    vpre = _gelu(proj[:, A_WIDTH:2 * A_WIDTH])
    p = proj[:, 2 * A_WIDTH:]
    mu = jnp.mean(vpre, -1, keepdims=True)
    xc = vpre - mu
    v = xc * lax.rsqrt(jnp.mean(xc * xc, -1, keepdims=True) + EPS) * vgain_ref[...]

    if sample:
        v_ref[0] = v
        nb = tm // shift
        for g in range(A_GROUPS):
            cs = slice(g * A_GD, (g + 1) * A_GD)
            for i in range(nb):
                acc = None
                for j in range(i + 1):
                    term = ws_ref[(g * nb + i) * nb + j] * v[j * shift:(j + 1) * shift, cs]
                    acc = term if acc is None else acc + term
                sg = acc + bs_ref[g * nb + i]
                a = u[i * shift:(i + 1) * shift, cs] * sg
                cat_ref[i * shift:(i + 1) * shift, cs] = a.astype(BF16)
    else:
        vb = v.astype(BF16)
        nc = tm // CHUNK
        for g in range(A_GROUPS):
            cs = slice(g * A_GD, (g + 1) * A_GD)
            rhs = jnp.concatenate([vb[c * CHUNK:(c + 1) * CHUNK, cs] for c in range(nc)], axis=1)
            sg = jnp.dot(wtril_ref[g], rhs, preferred_element_type=F32)
            for c in range(nc):
                rs = slice(c * CHUNK, (c + 1) * CHUNK)
                a = u[rs, cs] * (sg[:, c * A_GD:(c + 1) * A_GD] + bsb_ref[g])
                cat_ref[rs, cs] = a.astype(BF16)

    pbuf_ref[0:pfx, :] = carry_ref[...]
    pbuf_ref[pfx:pfx + tm, :] = p
    if tm >= pfx:
        carry_ref[...] = p[tm - pfx:, :]
    row = lax.broadcasted_iota(jnp.int32, (tm, B_GD), 0)
    pos = pos0 + (t * tm + row if shift == 1 else row // shift)
    pooled = []
    for gi, w in enumerate(POOL_WINDOWS):
        cs = slice(gi * B_GD, (gi + 1) * B_GD)
        acc = p[:, cs]
        for k in range(1, w):
            acc = acc + pbuf_ref[pfx - k * shift:pfx - k * shift + tm, cs]
        cnt = jnp.minimum(pos + 1, w).astype(F32)
        pooled.append(acc / cnt - p[:, cs])
    pooled = jnp.concatenate(pooled, axis=1).astype(BF16)
    b_out = jnp.dot(pooled, pwbd_ref[...], preferred_element_type=F32) * pscale_ref[...]
    cat_ref[:, A_WIDTH:] = b_out.astype(BF16)

    y = jnp.dot(cat_ref[...], wout_ref[...], preferred_element_type=F32)
    o_ref[0] = x + y

    if sample:
        ptail_ref[0] = p
    else:
        @pl.when(t == nt - 1)
        def _():
            ptail_ref[0] = carry_ref[...]


def _mix_ab(x, ctx, ln_g, w_in, v_gain, w_s, b_s, pool_w, pool_scale, w_out, *, tm, pfx, shift,
            pos0, sample):
    B, T, D = x.shape
    nb = T // shift if sample else SUBLANES
    tril = jnp.tril(jnp.ones((CHUNK, CHUNK), bool))
    wtril = jnp.where(tril, w_s, 0.0).astype(BF16)
    bsb = jnp.broadcast_to(b_s[:, :, None], (A_GROUPS, CHUNK, A_GD)).astype(F32)
    pwbd = jax.scipy.linalg.block_diag(*[pool_w[i] for i in range(len(POOL_WINDOWS))]).astype(BF16)
    ptail_rows = tm if sample else pfx
    out_shape = [jax.ShapeDtypeStruct((B, T, D), F32),
                 jax.ShapeDtypeStruct((B, ptail_rows, B_WIDTH), F32)]
    out_specs = [pl.BlockSpec((1, tm, D), lambda b, t: (b, t, 0)),
                 pl.BlockSpec((1, ptail_rows, B_WIDTH), lambda b, t: (b, 0, 0))]
    if sample:
        out_shape.append(jax.ShapeDtypeStruct((B, T, A_WIDTH), F32))
        out_specs.append(pl.BlockSpec((1, tm, A_WIDTH), lambda b, t: (b, t, 0)))
    smem = pl.BlockSpec(memory_space=pltpu.SMEM)
    body = functools.partial(_ab_body, tm=tm, pfx=pfx, shift=shift, pos0=pos0, sample=sample)
    return pl.pallas_call(
        body,
        grid=(B, T // tm),
        in_specs=[pl.BlockSpec((1, tm, D), lambda b, t: (b, t, 0)),
                  _const_spec((1, D)),
                  _const_spec((D, AB_IN)),
                  _const_spec((1, A_WIDTH)),
                  _const_spec((A_GROUPS, CHUNK, CHUNK)),
                  _const_spec((A_GROUPS, CHUNK, A_GD)),
                  smem, smem,
                  _const_spec((B_WIDTH, B_WIDTH)),
                  _const_spec((1, B_WIDTH)),
                  _const_spec((A_WIDTH + B_WIDTH, D)),
                  pl.BlockSpec((1, pfx, B_WIDTH), lambda b, t: (b, 0, 0))],
        out_specs=out_specs,
        out_shape=out_shape,
        scratch_shapes=[pltpu.VMEM((pfx + tm, B_WIDTH), F32),
                        pltpu.VMEM((pfx, B_WIDTH), F32),
                        pltpu.VMEM((tm, A_WIDTH + B_WIDTH), BF16)],
        compiler_params=_params(("arbitrary", "arbitrary")),
        name="mix_ab_sample" if sample else "mix_ab_prompt",
    )(x, ln_g.reshape(1, D), w_in.astype(BF16), v_gain.reshape(1, A_WIDTH), wtril, bsb,
      w_s[:, :nb, :nb].reshape(-1), b_s[:, :nb].reshape(-1), pwbd, pool_scale.reshape(1, B_WIDTH), w_out.astype(BF16), ctx)


def _memkv_body(mem_ref, g_ref, wkv_ref, kg_ref, k_ref, v_ref):
    h = _rms(mem_ref[0], g_ref[0]).astype(BF16)
    m = jnp.dot(h, wkv_ref[0], preferred_element_type=F32)
    ks = []
    for hd in range(MEM_HEADS):
        ks.append(_rms(m[:, hd * MEM_HD:(hd + 1) * MEM_HD], kg_ref[0]))
    k_ref[0, 0] = jnp.concatenate(ks, axis=1)
    v_ref[0, 0] = m[:, MEM_WIDTH:]


def _mem_kv(mem, ln_memkv, w_kv, k_gain):
    B = mem.shape[0]
    depth = w_kv.shape[0]
    shp = jax.ShapeDtypeStruct((depth, B, N_MEM, MEM_WIDTH), F32)
    return pl.pallas_call(
        _memkv_body,
        grid=(depth, B),
        in_specs=[pl.BlockSpec((1, N_MEM, D_MODEL), lambda l, b: (b, 0, 0)),
                  pl.BlockSpec((1, 1, D_MODEL), lambda l, b: (l, 0, 0)),
                  pl.BlockSpec((1, D_MODEL, 2 * MEM_WIDTH), lambda l, b: (l, 0, 0)),
                  pl.BlockSpec((1, 1, MEM_HD), lambda l, b: (l, 0, 0))],
        out_specs=[pl.BlockSpec((1, 1, N_MEM, MEM_WIDTH), lambda l, b: (l, b, 0, 0)),
                   pl.BlockSpec((1, 1, N_MEM, MEM_WIDTH), lambda l, b: (l, b, 0, 0))],
        out_shape=[shp, shp],
        compiler_params=_params(("arbitrary", "arbitrary")),
        name="mem_kv",
    )(mem, ln_memkv.reshape(depth, 1, D_MODEL), w_kv.astype(BF16), k_gain.reshape(depth, 1, MEM_HD))


def _mem_attn_body(x_ref, g_ref, wq_ref, qg_ref, k_ref, v_ref, wo_ref, o_ref, *, nb, tq):
    x = x_ref[...].reshape(nb * tq, D_MODEL)
    h = _rms(x, g_ref[...]).astype(BF16)
    q = jnp.dot(h, wq_ref[...], preferred_element_type=F32)
    scale = np.float32(MEM_HD ** -0.5)
    outs = []
    for hd in range(MEM_HEADS):
        cs = slice(hd * MEM_HD, (hd + 1) * MEM_HD)
        qh = _rms(q[:, cs], qg_ref[...]).astype(BF16)
        kh = k_ref[:, :, cs].astype(BF16)
        vh = v_ref[:, :, cs].astype(BF16)
        if nb == 1:
            s = lax.dot_general(qh, kh[0], (((1,), (1,)), ((), ())), preferred_element_type=F32)
        else:
            s = jnp.einsum('bqd,bkd->bqk', qh.reshape(nb, tq, MEM_HD), kh,
                           preferred_element_type=F32)
        s = s * scale
        m = jnp.max(s, -1, keepdims=True)
        e = jnp.exp(s - m)
        r = 1.0 / jnp.sum(e, -1, keepdims=True)
        if nb == 1:
            o = jnp.dot(e.astype(BF16), vh[0], preferred_element_type=F32)
        else:
            o = jnp.einsum('bqk,bkd->bqd', e.astype(BF16), vh, preferred_element_type=F32)
        o = (o * r).reshape(nb * tq, MEM_HD)
        outs.append(o.astype(BF16))
    o = jnp.concatenate(outs, axis=1)
    y = jnp.dot(o, wo_ref[...], preferred_element_type=F32)
    o_ref[...] = (x + y).reshape(o_ref.shape)


def _mem_attn(x, k, v, ln_g, w_q, q_gain, w_o, *, nb, tq):
    N, T, D = x.shape
    body = functools.partial(_mem_attn_body, nb=nb, tq=tq)
    return pl.pallas_call(
        body,
        grid=(N // nb, T // tq),
        in_specs=[pl.BlockSpec((nb, tq, D), lambda b, t: (b, t, 0)),
                  _const_spec((1, D)),
                  _const_spec((D, MEM_WIDTH)),
                  _const_spec((1, MEM_HD)),
                  pl.BlockSpec((nb, N_MEM, MEM_WIDTH), lambda b, t: (b, 0, 0)),
                  pl.BlockSpec((nb, N_MEM, MEM_WIDTH), lambda b, t: (b, 0, 0)),
                  _const_spec((MEM_WIDTH, D))],
        out_specs=pl.BlockSpec((nb, tq, D), lambda b, t: (b, t, 0)),
        out_shape=jax.ShapeDtypeStruct((N, T, D), F32),
        compiler_params=_params(("arbitrary", "arbitrary")),
        name="mem_attn_b%d" % nb,
    )(x, ln_g.reshape(1, D), w_q.astype(BF16), q_gain.reshape(1, MEM_HD), k, v, w_o.astype(BF16))


def _ffn_body(x_ref, g_ref, wup_ref, cw_ref, cb_ref, wdn_ref, ctx_ref, o_ref, tail_ref,
              h_ref, gbuf_ref, carry_ref, act_ref, *, tm, pfx, shift):
    t = pl.program_id(1)
    nt = pl.num_programs(1)
    fc = FFN_CHUNK

    @pl.when(t == 0)
    def _():
        carry_ref[...] = ctx_ref[0]

    x = x_ref[0]
    h_ref[...] = _rms(x, g_ref[...]).astype(BF16)
    for c in range(FFN_NCH):
        cs = slice(c * fc, (c + 1) * fc)
        up = jnp.dot(h_ref[...], wup_ref[:, 2 * c * fc:2 * (c + 1) * fc],
                     preferred_element_type=F32)
        g = up[:, :fc]
        u = up[:, fc:]
        gbuf_ref[0:pfx, :] = carry_ref[:, cs]
        gbuf_ref[pfx:pfx + tm, :] = g
        carry_ref[:, cs] = g[tm - pfx:, :]
        gc = cb_ref[:, cs] + cw_ref[0:1, cs] * gbuf_ref[pfx - 2 * shift:pfx - 2 * shift + tm, :]
        gc = gc + cw_ref[1:2, cs] * gbuf_ref[pfx - shift:pfx - shift + tm, :]
        gc = gc + cw_ref[2:3, cs] * g
        act_ref[:, cs] = (_gelu(gc) * u).astype(BF16)
    y = jnp.dot(act_ref[...], wdn_ref[...], preferred_element_type=F32)
    o_ref[0] = x + y

    @pl.when(t == nt - 1)
    def _():
        tail_ref[0] = carry_ref[...]


def _conv_ffn(x, ctx, ln_g, w_up, conv_w, conv_b, w_down, *, tm, pfx, shift):
    B, T, D = x.shape
    fc = FFN_CHUNK
    wup = w_up.reshape(D, 2, FFN_NCH, fc).transpose(0, 2, 1, 3).reshape(D, 2 * D_FF).astype(BF16)
    body = functools.partial(_ffn_body, tm=tm, pfx=pfx, shift=shift)
    return pl.pallas_call(
        body,
        grid=(B, T // tm),
        in_specs=[pl.BlockSpec((1, tm, D), lambda b, t: (b, t, 0)),
                  _const_spec((1, D)),
                  _const_spec((D, 2 * D_FF)),
                  _const_spec((CONV_W, D_FF)),
                  _const_spec((1, D_FF)),
                  _const_spec((D_FF, D)),
                  pl.BlockSpec((1, pfx, D_FF), lambda b, t: (b, 0, 0))],
        out_specs=[pl.BlockSpec((1, tm, D), lambda b, t: (b, t, 0)),
                   pl.BlockSpec((1, pfx, D_FF), lambda b, t: (b, 0, 0))],
        out_shape=[jax.ShapeDtypeStruct((B, T, D), F32),
                   jax.ShapeDtypeStruct((B, pfx, D_FF), F32)],
        scratch_shapes=[pltpu.VMEM((tm, D), BF16),
                        pltpu.VMEM((pfx + tm, fc), F32),
                        pltpu.VMEM((pfx, D_FF), F32),
                        pltpu.VMEM((tm, D_FF), BF16)],
        compiler_params=_params(("arbitrary", "arbitrary")),
        name="conv_ffn_s%d" % shift,
    )(x, ln_g.reshape(1, D), wup, conv_w, conv_b.reshape(1, D_FF), w_down.astype(BF16), ctx)


def _qkv_body(x_ref, g_ref, w_ref, qg_ref, kg_ref, cos_ref, sin_ref, q_ref, k_ref, v_ref, *, tm):
    x = x_ref[0]
    h = _rms(x, g_ref[...]).astype(BF16)
    qkv = jnp.dot(h, w_ref[...], preferred_element_type=F32)
    lane = lax.broadcasted_iota(jnp.int32, (tm, LANES), 1)
    lo = lane < HEAD_DIM
    first = (lane & (HEAD_DIM - 1)) < (ROT_DIM // 2)
    cosv = cos_ref[...]
    sinv = sin_ref[...]
    nq = N_HEADS * HEAD_DIM // LANES
    nk = N_KV * HEAD_DIM // LANES
    for j in range(nq + nk):
        col = qkv[:, j * LANES:(j + 1) * LANES]
        sq = col * col
        s_lo = jnp.sum(jnp.where(lo, sq, 0.0), -1, keepdims=True)
        s_hi = jnp.sum(jnp.where(lo, 0.0, sq), -1, keepdims=True)
        ms = jnp.where(lo, s_lo, s_hi) * np.float32(1.0 / HEAD_DIM)
        gain = qg_ref[...] if j < nq else kg_ref[...]
        y = col * lax.rsqrt(ms + EPS) * gain
        fwd = pltpu.roll(y, ROT_DIM // 2, 1)
        bwd = pltpu.roll(y, LANES - ROT_DIM // 2, 1)
        out = y * cosv + jnp.where(first, bwd, fwd) * sinv
        if j < nq:
            q_ref[0, :, j * LANES:(j + 1) * LANES] = (out * np.float32(HEAD_DIM ** -0.5)).astype(BF16)
        else:
            k_ref[0, :, (j - nq) * LANES:(j - nq + 1) * LANES] = out
    v_ref[0] = qkv[:, (N_HEADS + N_KV) * HEAD_DIM:]


def _rope_tables(pos):
    half = ROT_DIM // 2
    inv = ROPE_THETA ** (-jnp.arange(half, dtype=F32) / half)
    ang = pos.astype(F32)[:, None] * inv[None, :]
    cos, sin = jnp.cos(ang), jnp.sin(ang)
    n = pos.shape[0]
    pad = HEAD_DIM - ROT_DIM
    c = jnp.concatenate([cos, cos, jnp.ones((n, pad), F32)], -1)
    s = jnp.concatenate([-sin, sin, jnp.zeros((n, pad), F32)], -1)
    return jnp.concatenate([c, c], -1), jnp.concatenate([s, s], -1)


def _swa_qkv(x, pos, ln_g, w_qkv, q_gain, k_gain, *, tm):
    B, T, D = x.shape
    cos, sin = _rope_tables(pos)
    kvw = N_KV * HEAD_DIM
    body = functools.partial(_qkv_body, tm=tm)
    return pl.pallas_call(
        body,
        grid=(B, T // tm),
        in_specs=[pl.BlockSpec((1, tm, D), lambda b, t: (b, t, 0)),
                  _const_spec((1, D)),
                  _const_spec((D, QKV_OUT)),
                  _const_spec((1, LANES)),
                  _const_spec((1, LANES)),
                  pl.BlockSpec((tm, LANES), lambda b, t: (t, 0)),
                  pl.BlockSpec((tm, LANES), lambda b, t: (t, 0))],
        out_specs=[pl.BlockSpec((1, tm, N_HEADS * HEAD_DIM), lambda b, t: (b, t, 0)),
                   pl.BlockSpec((1, tm, kvw), lambda b, t: (b, t, 0)),
                   pl.BlockSpec((1, tm, kvw), lambda b, t: (b, t, 0))],
        out_shape=[jax.ShapeDtypeStruct((B, T, N_HEADS * HEAD_DIM), BF16),
                   jax.ShapeDtypeStruct((B, T, kvw), F32),
                   jax.ShapeDtypeStruct((B, T, kvw), F32)],
        compiler_params=_params(("arbitrary", "arbitrary")),
        name="swa_qkv",
    )(x, ln_g.reshape(1, D), w_qkv.astype(BF16), jnp.tile(q_gain, 2).reshape(1, LANES),
      jnp.tile(k_gain, 2).reshape(1, LANES), cos, sin)


def _both_halves(a):
    lane = lax.broadcasted_iota(jnp.int32, a.shape, a.ndim - 1)
    lo = lane < HEAD_DIM
    r = pltpu.roll(a, HEAD_DIM, a.ndim - 1)
    return jnp.where(lo, a, r), jnp.where(lo, r, a)


def _swa_prompt_body(sink_ref, q_ref, kc_ref, kp_ref, vc_ref, vp_ref, o_ref, *, tq):
    t = pl.program_id(1)
    kall = jnp.concatenate([kp_ref[0], kc_ref[0]], axis=0)
    vall = jnp.concatenate([vp_ref[0], vc_ref[0]], axis=0)
    kb, vb = [], []
    for m in range(N_KV * HEAD_DIM // LANES):
        k0, k1 = _both_halves(kall[:, m * LANES:(m + 1) * LANES])
        v0, v1 = _both_halves(vall[:, m * LANES:(m + 1) * LANES])
        kb += [k0.astype(BF16), k1.astype(BF16)]
        vb += [v0.astype(BF16), v1.astype(BF16)]
    i = lax.broadcasted_iota(jnp.int32, (WINDOW, 2 * WINDOW), 0)
    s_idx = lax.broadcasted_iota(jnp.int32, (WINDOW, 2 * WINDOW), 1)
    band = (s_idx > i) & (s_idx <= i + WINDOW)
    first_ok = band & (s_idx >= jnp.where(t > 0, 0, WINDOW))
    lane = lax.broadcasted_iota(jnp.int32, (WINDOW, LANES), 1)
    lo = lane < HEAD_DIM
    zero = jnp.zeros((WINDOW, LANES), BF16)
    for qb in range(tq // WINDOW):
        rs = slice(qb * WINDOW, (qb + 1) * WINDOW)
        ks = slice(qb * WINDOW, (qb + 2) * WINDOW)
        mask = first_ok if qb == 0 else band
        for kv in range(N_KV):
            lhs = []
            for j in (2 * kv, 2 * kv + 1):
                qp = q_ref[0, rs, j * LANES:(j + 1) * LANES]
                lhs += [jnp.where(lo, qp, zero), jnp.where(lo, zero, qp)]
            lhs = jnp.concatenate(lhs, axis=0)
            s = lax.dot_general(lhs, kb[kv][ks], (((1,), (1,)), ((), ())),
                                preferred_element_type=F32)
            pvs = []
            for hh in range(GQA):
                sink = sink_ref[kv * GQA + hh]
                sh = jnp.where(mask, s[hh * WINDOW:(hh + 1) * WINDOW], -jnp.inf)
                mx = jnp.maximum(jnp.max(sh, -1, keepdims=True), sink)
                e = jnp.exp(sh - mx)
                r = 1.0 / (jnp.sum(e, -1, keepdims=True) + jnp.exp(sink - mx))
                pv = jnp.dot(e.astype(BF16), vb[kv][ks], preferred_element_type=F32)
                pvs.append(pv * r)
            o0 = jnp.where(lo, pvs[0], pvs[1])
            o1 = jnp.where(lo, pvs[2], pvs[3])
            o_ref[0, rs, 2 * kv * LANES:(2 * kv + 2) * LANES] = jnp.concatenate([o0, o1], 1).astype(BF16)


def _swa_prompt_attn(q, k, v, sinks, *, tq):
    B, T, _ = q.shape
    kvw = N_KV * HEAD_DIM
    r = tq // WINDOW
    body = functools.partial(_swa_prompt_body, tq=tq)
    cur = lambda b, t: (b, t, 0)
    prev = lambda b, t: (b, jnp.maximum(t * r - 1, 0), 0)
    return pl.pallas_call(
        body,
        grid=(B, T // tq),
        in_specs=[pl.BlockSpec(memory_space=pltpu.SMEM),
                  pl.BlockSpec((1, tq, N_HEADS * HEAD_DIM), cur),
                  pl.BlockSpec((1, tq, kvw), cur),
                  pl.BlockSpec((1, WINDOW, kvw), prev),
                  pl.BlockSpec((1, tq, kvw), cur),
                  pl.BlockSpec((1, WINDOW, kvw), prev)],
        out_specs=pl.BlockSpec((1, tq, N_HEADS * HEAD_DIM), cur),
        out_shape=jax.ShapeDtypeStruct((B, T, N_HEADS * HEAD_DIM), BF16),
        compiler_params=_params(("arbitrary", "arbitrary")),
        name="swa_prompt_attn",
    )(sinks, q, k, k, v, v)


def _swa_sample_body(sink_ref, q_ref, kn_ref, vn_ref, kc_ref, vc_ref, o_ref, ko_ref, vo_ref, *, sb, ts):
    pad = jnp.zeros((sb, WINDOW - ts, N_KV * HEAD_DIM), F32)
    kall = jnp.concatenate([kc_ref[...], kn_ref[...], pad], axis=1)
    vall = jnp.concatenate([vc_ref[...], vn_ref[...], pad], axis=1)
    ko_ref[...] = jnp.concatenate([kc_ref[:, ts:, :], kn_ref[...]], axis=1)
    vo_ref[...] = jnp.concatenate([vc_ref[:, ts:, :], vn_ref[...]], axis=1)
    nkeys = 2 * WINDOW
    rows = GQA * ts
    r_idx = lax.broadcasted_iota(jnp.int32, (sb, rows, nkeys), 1)
    s_idx = lax.broadcasted_iota(jnp.int32, (sb, rows, nkeys), 2)
    i = lax.rem(r_idx, ts)
    mask = (s_idx > i) & (s_idx <= i + WINDOW)
    hrow = lax.div(lax.broadcasted_iota(jnp.int32, (sb, rows, 1), 1), ts)
    lane = lax.broadcasted_iota(jnp.int32, (sb, ts, LANES), 2)
    lo = lane < HEAD_DIM
    qf = q_ref[...]
    for m in range(N_KV * HEAD_DIM // LANES):
        kpair = _both_halves(kall[:, :, m * LANES:(m + 1) * LANES])
        vpair = _both_halves(vall[:, :, m * LANES:(m + 1) * LANES])
        for half in range(2):
            kv = 2 * m + half
            kb = kpair[half].astype(BF16)
            vb = vpair[half].astype(BF16)
            lhs = []
            for j in (2 * kv, 2 * kv + 1):
                qp = qf[:, :, j * LANES:(j + 1) * LANES]
                lhs += [jnp.where(lo, qp, 0.0), jnp.where(lo, 0.0, qp)]
            lhs = jnp.concatenate(lhs, axis=1).astype(BF16)
            s = jnp.einsum('bqd,bkd->bqk', lhs, kb, preferred_element_type=F32)
            s = jnp.where(mask, s, -jnp.inf)
            sink = jnp.zeros((sb, rows, 1), F32)
            for hh in range(GQA):
                sink = jnp.where(hrow == hh, sink_ref[kv * GQA + hh], sink)
            mx = jnp.maximum(jnp.max(s, -1, keepdims=True), sink)
            e = jnp.exp(s - mx)
            r = 1.0 / (jnp.sum(e, -1, keepdims=True) + jnp.exp(sink - mx))
            pv = jnp.einsum('bqk,bkd->bqd', e.astype(BF16), vb, preferred_element_type=F32) * r
            o0 = jnp.where(lo, pv[:, 0:ts], pv[:, ts:2 * ts])
            o1 = jnp.where(lo, pv[:, 2 * ts:3 * ts], pv[:, 3 * ts:4 * ts])
            o_ref[:, :, 2 * kv * LANES:(2 * kv + 2) * LANES] = jnp.concatenate([o0, o1], 2)


def _swa_sample_attn(q, k_new, v_new, k_ctx, v_ctx, sinks, *, sb):
    N, ts, _ = q.shape
    kvw = N_KV * HEAD_DIM
    body = functools.partial(_swa_sample_body, sb=sb, ts=ts)
    blk = lambda n: (n, 0, 0)
    return pl.pallas_call(
        body,
        grid=(N // sb,),
        in_specs=[pl.BlockSpec(memory_space=pltpu.SMEM),
                  pl.BlockSpec((sb, ts, N_HEADS * HEAD_DIM), blk),
                  pl.BlockSpec((sb, ts, kvw), blk),
                  pl.BlockSpec((sb, ts, kvw), blk),
                  pl.BlockSpec((sb, WINDOW, kvw), blk),
                  pl.BlockSpec((sb, WINDOW, kvw), blk)],
        out_specs=[pl.BlockSpec((sb, ts, N_HEADS * HEAD_DIM), blk),
                   pl.BlockSpec((sb, WINDOW, kvw), blk),
                   pl.BlockSpec((sb, WINDOW, kvw), blk)],
        out_shape=[jax.ShapeDtypeStruct((N, ts, N_HEADS * HEAD_DIM), F32),
                   jax.ShapeDtypeStruct((N, WINDOW, kvw), F32),
                   jax.ShapeDtypeStruct((N, WINDOW, kvw), F32)],
        compiler_params=_params(("arbitrary",)),
        name="swa_sample_attn",
    )(sinks, q, k_new, v_new, k_ctx, v_ctx)


def _proj_res_body(x_ref, a_ref, w_ref, o_ref):
    o_ref[...] = x_ref[...] + jnp.dot(a_ref[...].astype(BF16), w_ref[...], preferred_element_type=F32)


def _proj_res(x, a, w, *, tm):
    R, D = x.shape
    K = a.shape[1]
    return pl.pallas_call(
        _proj_res_body,
        grid=(R // tm,),
        in_specs=[pl.BlockSpec((tm, D), lambda i: (i, 0)),
                  pl.BlockSpec((tm, K), lambda i: (i, 0)),
                  _const_spec((K, D))],
        out_specs=pl.BlockSpec((tm, D), lambda i: (i, 0)),
        out_shape=jax.ShapeDtypeStruct((R, D), F32),
        compiler_params=_params(("arbitrary",)),
        name="proj_res",
    )(x, a, w.astype(BF16))


def kernel(x_prompt, x_sample, cache_pool, cache_swa_k, cache_swa_v, cache_mem_k, cache_mem_v, cache_ffn_conv, mem_prompt, ln_mix, ln_mem, ln_memkv, ln_ffn, ab_w_in, ab_v_gain, ab_w_s, ab_b_s, ab_pool_w, ab_pool_scale, ab_w_out, c_w_qkv, c_q_gain, c_k_gain, c_sinks, c_w_o, m_w_q, m_w_kv, m_q_gain, m_k_gain, m_w_o, f_w_up, f_conv_w, f_conv_b, f_w_down):
    xp = x_prompt
    Bp, Tp, D = xp.shape
    Bs, Ts, _ = x_sample.shape
    depth = ln_mix.shape[0]
    tm = PROMPT_TILE
    kvw = N_KV * HEAD_DIM

    sblk = SAMPLE_BLOCK
    nblk = Bs // sblk

    def to_tm(a):
        n, c = a.shape[1], a.shape[2]
        return a.reshape(nblk, sblk, n, c).transpose(0, 2, 1, 3).reshape(nblk, n * sblk, c)

    def to_sm(a):
        n, c = a.shape[1] // sblk, a.shape[2]
        return a.reshape(nblk, n, sblk, c).transpose(0, 2, 1, 3).reshape(Bs, n, c)

    xs = x_sample
    pos_p = jnp.arange(Tp)
    pos_s = PAST_LEN + jnp.arange(Ts)

    mem_k, mem_v = _mem_kv(mem_prompt, ln_memkv, m_w_kv, m_k_gain)

    pool_p, pool_s, chunk_v_s = [], [], []
    swa_kp, swa_vp, swa_ks, swa_vs = [], [], [], []
    conv_p, conv_s = [], []
    ffn_pfx_p = SUBLANES
    ffn_pfx_s = (CONV_W - 1) * sblk
    for l in range(depth):
        j = l // 2
        if l % 2 == 0:
            pfx_p = 2 * SUBLANES
            zp = jnp.zeros((Bp, pfx_p, B_WIDTH), F32)
            xp, ptail = _mix_ab(xp, zp, ln_mix[l], ab_w_in[j], ab_v_gain[j], ab_w_s[j], ab_b_s[j],
                                ab_pool_w[j], ab_pool_scale[j], ab_w_out[j],
                                tm=tm, pfx=pfx_p, shift=1, pos0=0, sample=False)
            pool_p.append(ptail[:, pfx_p - POOL_CTX:])
            ctx_s = to_tm(cache_pool[j])
            xs_tm, p_s, v_s = _mix_ab(to_tm(xs), ctx_s, ln_mix[l], ab_w_in[j], ab_v_gain[j],
                                      ab_w_s[j], ab_b_s[j], ab_pool_w[j], ab_pool_scale[j],
                                      ab_w_out[j], tm=Ts * sblk, pfx=POOL_CTX * sblk, shift=sblk,
                                      pos0=PAST_LEN, sample=True)
            xs = to_sm(xs_tm)
            pool_s.append(jnp.concatenate([cache_pool[j], to_sm(p_s)], 1)[:, -POOL_CTX:])
            chunk_v_s.append(to_sm(v_s))
        else:
            q, k, v = _swa_qkv(xp, pos_p, ln_mix[l], c_w_qkv[j], c_q_gain[j], c_k_gain[j], tm=tm)
            o = _swa_prompt_attn(q, k, v, c_sinks[j], tq=tm)
            xp = _proj_res(xp.reshape(Bp * Tp, D), o.reshape(Bp * Tp, N_HEADS * HEAD_DIM),
                           c_w_o[j], tm=tm).reshape(Bp, Tp, D)
            swa_kp.append(k[:, -WINDOW:].reshape(Bp, WINDOW, N_KV, HEAD_DIM))
            swa_vp.append(v[:, -WINDOW:].reshape(Bp, WINDOW, N_KV, HEAD_DIM))
            qs, ks_, vs_ = _swa_qkv(xs.reshape(1, Bs * Ts, D), jnp.tile(pos_s, Bs), ln_mix[l],
                                    c_w_qkv[j], c_q_gain[j], c_k_gain[j], tm=Bs * Ts)
            os_, kn, vn = _swa_sample_attn(qs.reshape(Bs, Ts, N_HEADS * HEAD_DIM).astype(F32),
                                           ks_.reshape(Bs, Ts, kvw), vs_.reshape(Bs, Ts, kvw),
                                           cache_swa_k[j].reshape(Bs, WINDOW, kvw),
                                           cache_swa_v[j].reshape(Bs, WINDOW, kvw), c_sinks[j], sb=16)
            xs = _proj_res(xs.reshape(Bs * Ts, D), os_.reshape(Bs * Ts, N_HEADS * HEAD_DIM),
                           c_w_o[j], tm=Bs * Ts).reshape(Bs, Ts, D)
            swa_ks.append(kn.reshape(Bs, WINDOW, N_KV, HEAD_DIM))
            swa_vs.append(vn.reshape(Bs, WINDOW, N_KV, HEAD_DIM))
        xp = _mem_attn(xp, mem_k[l], mem_v[l], ln_mem[l], m_w_q[l], m_q_gain[l], m_w_o[l],
                       nb=1, tq=tm)
        xs = _mem_attn(xs, cache_mem_k[l].reshape(Bs, N_MEM, MEM_WIDTH),
                       cache_mem_v[l].reshape(Bs, N_MEM, MEM_WIDTH), ln_mem[l], m_w_q[l],
                       m_q_gain[l], m_w_o[l], nb=8, tq=Ts)
        zc = jnp.zeros((Bp, ffn_pfx_p, D_FF), F32)
        xp, tail_p = _conv_ffn(xp, zc, ln_ffn[l], f_w_up[l], f_conv_w[l], f_conv_b[l], f_w_down[l],
                               tm=tm, pfx=ffn_pfx_p, shift=1)
        conv_p.append(tail_p[:, ffn_pfx_p - (CONV_W - 1):])
        xs_tm, tail_s = _conv_ffn(to_tm(xs), to_tm(cache_ffn_conv[l]), ln_ffn[l], f_w_up[l],
                                  f_conv_w[l], f_conv_b[l], f_w_down[l],
                                  tm=Ts * sblk, pfx=ffn_pfx_s, shift=sblk)
        xs = to_sm(xs_tm)
        conv_s.append(to_sm(tail_s))
    mk = mem_k.reshape(depth, Bp, N_MEM, MEM_HEADS, MEM_HD)
    mv = mem_v.reshape(depth, Bp, N_MEM, MEM_HEADS, MEM_HD)
    return (xp, xs,
            jnp.stack(pool_p), jnp.stack(pool_s), jnp.stack(chunk_v_s),
            jnp.stack(swa_kp), jnp.stack(swa_vp), jnp.stack(swa_ks), jnp.stack(swa_vs),
            mk, mv,
            jnp.stack(conv_p), jnp.stack(conv_s))
```

```python
import functools
import math

import jax
import jax.numpy as jnp
import numpy as np
from jax import lax
from jax.experimental import pallas as pl
from jax.experimental.pallas import tpu as pltpu

D_MODEL = 1024
PAST_LEN = 16384
CHUNK = 128
A_GROUPS = 4
A_GD = 128
A_WIDTH = A_GROUPS * A_GD
POOL_WINDOWS = (2, 4, 8, 16)
B_GD = 128
B_WIDTH = len(POOL_WINDOWS) * B_GD
POOL_CTX = max(POOL_WINDOWS) - 1
AB_IN = 2 * A_WIDTH + B_WIDTH
N_HEADS = 16
N_KV = 4
HEAD_DIM = 64
GQA = N_HEADS // N_KV
WINDOW = 128
ROT_DIM = HEAD_DIM // 4
ROPE_THETA = 500000.0
QKV_OUT = (N_HEADS + 2 * N_KV) * HEAD_DIM
N_MEM = 256
MEM_HEADS = 4
MEM_HD = 128
MEM_WIDTH = MEM_HEADS * MEM_HD
D_FF = 2816
CONV_W = 3
EPS = 1e-6

LANES = 128
SUBLANES = 8
VMEM_LIMIT = 56 * 1024 * 1024

F32 = jnp.float32
BF16 = jnp.bfloat16

PROMPT_TILE = 512
FFN_CHUNK = 256
FFN_NCH = D_FF // FFN_CHUNK
SAMPLE_BLOCK = 64


def _gelu(x):
    return 0.5 * x * (1.0 + lax.erf(x * np.float32(math.sqrt(0.5))))


def _rms(x, g):
    return x * lax.rsqrt(jnp.mean(x * x, -1, keepdims=True) + EPS) * g


def _const_spec(shape):
    zeros = (0,) * len(shape)
    return pl.BlockSpec(shape, lambda *_: zeros, pipeline_mode=pl.Buffered(1))


def _params(sem):
    return pltpu.CompilerParams(dimension_semantics=sem, vmem_limit_bytes=VMEM_LIMIT)


def _ab_body(x_ref, g_ref, win_ref, vgain_ref, wtril_ref, bsb_ref, ws_ref, bs_ref, pwbd_ref,
             pscale_ref, wout_ref, ctx_ref, *rest, tm, pfx, shift, pos0, sample):
    if sample:
        o_ref, ptail_ref, v_ref, pbuf_ref, carry_ref, cat_ref = rest
    else:
        o_ref, ptail_ref, pbuf_ref, carry_ref, cat_ref = rest
    t = pl.program_id(1)
    nt = pl.num_programs(1)

    @pl.when(t == 0)
    def _():
        carry_ref[...] = ctx_ref[0]

    x = x_ref[0]
    h = _rms(x, g_ref[...]).astype(BF16)
    proj = jnp.dot(h, win_ref[...], preferred_element_type=F32)
    u = _gelu(proj[:, :A_WIDTH])
    vpre = _gelu(proj[:, A_WIDTH:2 * A_WIDTH])
    p = proj[:, 2 * A_WIDTH:]
    mu = jnp.mean(vpre, -1, keepdims=True)
    xc = vpre - mu
    v = xc * lax.rsqrt(jnp.mean(xc * xc, -1, keepdims=True) + EPS) * vgain_ref[...]

    if sample:
        v_ref[0] = v
        nb = tm // shift
        for g in range(A_GROUPS):
            cs = slice(g * A_GD, (g + 1) * A_GD)
            for i in range(nb):
                acc = None
                for j in range(i + 1):
                    term = ws_ref[(g * nb + i) * nb + j] * v[j * shift:(j + 1) * shift, cs]
                    acc = term if acc is None else acc + term
                sg = acc + bs_ref[g * nb + i]
                a = u[i * shift:(i + 1) * shift, cs] * sg
                cat_ref[i * shift:(i + 1) * shift, cs] = a.astype(BF16)
    else:
        vb = v.astype(BF16)
        nc = tm // CHUNK
        for g in range(A_GROUPS):
            cs = slice(g * A_GD, (g + 1) * A_GD)
            rhs = jnp.concatenate([vb[c * CHUNK:(c + 1) * CHUNK, cs] for c in range(nc)], axis=1)
            sg = jnp.dot(wtril_ref[g], rhs, preferred_element_type=F32)
            for c in range(nc):
                rs = slice(c * CHUNK, (c + 1) * CHUNK)
                a = u[rs, cs] * (sg[:, c * A_GD:(c + 1) * A_GD] + bsb_ref[g])
                cat_ref[rs, cs] = a.astype(BF16)

    pbuf_ref[0:pfx, :] = carry_ref[...]
    pbuf_ref[pfx:pfx + tm, :] = p
    if tm >= pfx:
        carry_ref[...] = p[tm - pfx:, :]
    row = lax.broadcasted_iota(jnp.int32, (tm, B_GD), 0)
    pos = pos0 + (t * tm + row if shift == 1 else row // shift)
    pooled = []
    for gi, w in enumerate(POOL_WINDOWS):
        cs = slice(gi * B_GD, (gi + 1) * B_GD)
        acc = p[:, cs]
        for k in range(1, w):
            acc = acc + pbuf_ref[pfx - k * shift:pfx - k * shift + tm, cs]
        cnt = jnp.minimum(pos + 1, w).astype(F32)
        pooled.append(acc / cnt - p[:, cs])
    pooled = jnp.concatenate(pooled, axis=1).astype(BF16)
    b_out = jnp.dot(pooled, pwbd_ref[...], preferred_element_type=F32) * pscale_ref[...]
    cat_ref[:, A_WIDTH:] = b_out.astype(BF16)

    y = jnp.dot(cat_ref[...], wout_ref[...], preferred_element_type=F32)
    o_ref[0] = x + y

    if sample:
        ptail_ref[0] = p
    else:
        @pl.when(t == nt - 1)
        def _():
            ptail_ref[0] = carry_ref[...]


def _mix_ab(x, ctx, ln_g, w_in, v_gain, w_s, b_s, pool_w, pool_scale, w_out, *, tm, pfx, shift,
            pos0, sample):
    B, T, D = x.shape
    nb = T // shift if sample else SUBLANES
    tril = jnp.tril(jnp.ones((CHUNK, CHUNK), bool))
    wtril = jnp.where(tril, w_s, 0.0).astype(BF16)
    bsb = jnp.broadcast_to(b_s[:, :, None], (A_GROUPS, CHUNK, A_GD)).astype(F32)
    pwbd = jax.scipy.linalg.block_diag(*[pool_w[i] for i in range(len(POOL_WINDOWS))]).astype(BF16)
    ptail_rows = tm if sample else pfx
    out_shape = [jax.ShapeDtypeStruct((B, T, D), F32),
                 jax.ShapeDtypeStruct((B, ptail_rows, B_WIDTH), F32)]
    out_specs = [pl.BlockSpec((1, tm, D), lambda b, t: (b, t, 0)),
                 pl.BlockSpec((1, ptail_rows, B_WIDTH), lambda b, t: (b, 0, 0))]
    if sample:
        out_shape.append(jax.ShapeDtypeStruct((B, T, A_WIDTH), F32))
        out_specs.append(pl.BlockSpec((1, tm, A_WIDTH), lambda b, t: (b, t, 0)))
    smem = pl.BlockSpec(memory_space=pltpu.SMEM)
    body = functools.partial(_ab_body, tm=tm, pfx=pfx, shift=shift, pos0=pos0, sample=sample)
    return pl.pallas_call(
        body,
        grid=(B, T // tm),
        in_specs=[pl.BlockSpec((1, tm, D), lambda b, t: (b, t, 0)),
                  _const_spec((1, D)),
                  _const_spec((D, AB_IN)),
                  _const_spec((1, A_WIDTH)),
                  _const_spec((A_GROUPS, CHUNK, CHUNK)),
                  _const_spec((A_GROUPS, CHUNK, A_GD)),
                  smem, smem,
                  _const_spec((B_WIDTH, B_WIDTH)),
                  _const_spec((1, B_WIDTH)),
                  _const_spec((A_WIDTH + B_WIDTH, D)),
                  pl.BlockSpec((1, pfx, B_WIDTH), lambda b, t: (b, 0, 0))],
        out_specs=out_specs,
        out_shape=out_shape,
        scratch_shapes=[pltpu.VMEM((pfx + tm, B_WIDTH), F32),
                        pltpu.VMEM((pfx, B_WIDTH), F32),
                        pltpu.VMEM((tm, A_WIDTH + B_WIDTH), BF16)],
        compiler_params=_params(("arbitrary", "arbitrary")),
        name="mix_ab_sample" if sample else "mix_ab_prompt",
    )(x, ln_g.reshape(1, D), w_in.astype(BF16), v_gain.reshape(1, A_WIDTH), wtril, bsb,
      w_s[:, :nb, :nb].reshape(-1), b_s[:, :nb].reshape(-1), pwbd, pool_scale.reshape(1, B_WIDTH), w_out.astype(BF16), ctx)


def _memkv_body(mem_ref, g_ref, wkv_ref, kg_ref, k_ref, v_ref):
    h = _rms(mem_ref[0], g_ref[0]).astype(BF16)
    m = jnp.dot(h, wkv_ref[0], preferred_element_type=F32)
    ks = []
    for hd in range(MEM_HEADS):
        ks.append(_rms(m[:, hd * MEM_HD:(hd + 1) * MEM_HD], kg_ref[0]))
    k_ref[0, 0] = jnp.concatenate(ks, axis=1)
    v_ref[0, 0] = m[:, MEM_WIDTH:]


def _mem_kv(mem, ln_memkv, w_kv, k_gain):
    B = mem.shape[0]
    depth = w_kv.shape[0]
    shp = jax.ShapeDtypeStruct((depth, B, N_MEM, MEM_WIDTH), F32)
    return pl.pallas_call(
        _memkv_body,
        grid=(depth, B),
        in_specs=[pl.BlockSpec((1, N_MEM, D_MODEL), lambda l, b: (b, 0, 0)),
                  pl.BlockSpec((1, 1, D_MODEL), lambda l, b: (l, 0, 0)),
                  pl.BlockSpec((1, D_MODEL, 2 * MEM_WIDTH), lambda l, b: (l, 0, 0)),
                  pl.BlockSpec((1, 1, MEM_HD), lambda l, b: (l, 0, 0))],
        out_specs=[pl.BlockSpec((1, 1, N_MEM, MEM_WIDTH), lambda l, b: (l, b, 0, 0)),
                   pl.BlockSpec((1, 1, N_MEM, MEM_WIDTH), lambda l, b: (l, b, 0, 0))],
        out_shape=[shp, shp],
        compiler_params=_params(("arbitrary", "arbitrary")),
        name="mem_kv",
    )(mem, ln_memkv.reshape(depth, 1, D_MODEL), w_kv.astype(BF16), k_gain.reshape(depth, 1, MEM_HD))


def _mem_attn_prompt_body(x_ref, g_ref, wq_ref, qg_ref, k_ref, v_ref, wo_ref, o_ref):
    x = x_ref[0]
    h = _rms(x, g_ref[...]).astype(BF16)
    q = jnp.dot(h, wq_ref[...], preferred_element_type=F32)
    scale = np.float32(MEM_HD ** -0.5)
    outs = []
    for hd in range(MEM_HEADS):
        cs = slice(hd * MEM_HD, (hd + 1) * MEM_HD)
        qh = _rms(q[:, cs], qg_ref[...]).astype(BF16)
        kh = k_ref[0, 0, :, cs].astype(BF16)
        vh = v_ref[0, 0, :, cs].astype(BF16)
        s = lax.dot_general(qh, kh, (((1,), (1,)), ((), ())), preferred_element_type=F32) * scale
        m = jnp.max(s, -1, keepdims=True)
        e = jnp.exp(s - m)
        r = 1.0 / jnp.sum(e, -1, keepdims=True)
        o = jnp.dot(e.astype(BF16), vh, preferred_element_type=F32) * r
        outs.append(o.astype(BF16))
    o = jnp.concatenate(outs, axis=1)
    o_ref[0] = x + jnp.dot(o, wo_ref[...], preferred_element_type=F32)


def _mem_attn_prompt(x, k, v, layer, ln_g, w_q, q_gain, w_o, *, tq):
    B, T, D = x.shape
    return pl.pallas_call(
        _mem_attn_prompt_body,
        grid=(B, T // tq),
        in_specs=[pl.BlockSpec((1, tq, D), lambda b, t: (b, t, 0)),
                  _const_spec((1, D)),
                  _const_spec((D, MEM_WIDTH)),
                  _const_spec((1, MEM_HD)),
                  pl.BlockSpec((1, 1, N_MEM, MEM_WIDTH), lambda b, t: (layer, b, 0, 0)),
                  pl.BlockSpec((1, 1, N_MEM, MEM_WIDTH), lambda b, t: (layer, b, 0, 0)),
                  _const_spec((MEM_WIDTH, D))],
        out_specs=pl.BlockSpec((1, tq, D), lambda b, t: (b, t, 0)),
        out_shape=jax.ShapeDtypeStruct((B, T, D), F32),
        compiler_params=_params(("arbitrary", "arbitrary")),
        name="mem_attn_prompt",
    )(x, ln_g.reshape(1, D), w_q.astype(BF16), q_gain.reshape(1, MEM_HD), k, v, w_o.astype(BF16))


def _mem_attn_sample_body(x_ref, g_ref, wq_ref, qg_ref, k_ref, v_ref, wo_ref, o_ref, *, nb, tq):
    nkeys = N_MEM * MEM_HEADS
    rows = MEM_HEADS * tq
    x = x_ref[...].reshape(nb * tq, D_MODEL)
    h = _rms(x, g_ref[...]).astype(BF16)
    q = jnp.dot(h, wq_ref[...], preferred_element_type=F32)
    qs = [_rms(q[:, hd * MEM_HD:(hd + 1) * MEM_HD], qg_ref[...]).reshape(nb, tq, MEM_HD)
          for hd in range(MEM_HEADS)]
    qa = jnp.concatenate(qs, axis=1).astype(BF16)
    s = jnp.einsum('bqd,bkd->bqk', qa, k_ref[0].astype(BF16), preferred_element_type=F32)
    row_head = lax.div(lax.broadcasted_iota(jnp.int32, (rows, nkeys), 0), tq)
    key_head = lax.broadcasted_iota(jnp.int32, (rows, nkeys), 1) & (MEM_HEADS - 1)
    bias = jnp.where(row_head == key_head, 0.0, -jnp.inf).astype(F32)
    s = s * np.float32(MEM_HD ** -0.5) + bias[None]
    m = jnp.max(s, -1, keepdims=True)
    e = jnp.exp(s - m)
    r = 1.0 / jnp.sum(e, -1, keepdims=True)
    o = jnp.einsum('bqk,bkd->bqd', e.astype(BF16), v_ref[0].astype(BF16),
                   preferred_element_type=F32) * r
    o = jnp.concatenate([o[:, hd * tq:(hd + 1) * tq] for hd in range(MEM_HEADS)], axis=2)
    y = jnp.dot(o.reshape(nb * tq, MEM_WIDTH).astype(BF16), wo_ref[...], preferred_element_type=F32)
    o_ref[...] = (x + y).reshape(o_ref.shape)


def _mem_attn_sample(x, k_cache, v_cache, layer, ln_g, w_q, q_gain, w_o, *, nb):
    N, tq, D = x.shape
    depth = k_cache.shape[0]
    nkeys = N_MEM * MEM_HEADS
    k = k_cache.reshape(depth, N, nkeys, MEM_HD)
    v = v_cache.reshape(depth, N, nkeys, MEM_HD)
    body = functools.partial(_mem_attn_sample_body, nb=nb, tq=tq)
    return pl.pallas_call(
        body,
        grid=(N // nb,),
        in_specs=[pl.BlockSpec((nb, tq, D), lambda b: (b, 0, 0)),
                  _const_spec((1, D)),
                  _const_spec((D, MEM_WIDTH)),
                  _const_spec((1, MEM_HD)),
                  pl.BlockSpec((1, nb, nkeys, MEM_HD), lambda b: (layer, b, 0, 0)),
                  pl.BlockSpec((1, nb, nkeys, MEM_HD), lambda b: (layer, b, 0, 0)),
                  _const_spec((MEM_WIDTH, D))],
        out_specs=pl.BlockSpec((nb, tq, D), lambda b: (b, 0, 0)),
        out_shape=jax.ShapeDtypeStruct((N, tq, D), F32),
        compiler_params=_params(("arbitrary",)),
        name="mem_attn_sample",
    )(x, ln_g.reshape(1, D), w_q.astype(BF16), q_gain.reshape(1, MEM_HD), k, v, w_o.astype(BF16))


def _ffn_body(x_ref, g_ref, wup_ref, cw_ref, cb_ref, wdn_ref, ctx_ref, o_ref, tail_ref,
              h_ref, gbuf_ref, carry_ref, act_ref, *, tm, pfx, shift):
    t = pl.program_id(1)
    nt = pl.num_programs(1)
    fc = FFN_CHUNK

    @pl.when(t == 0)
    def _():
        carry_ref[...] = ctx_ref[0]

    x = x_ref[0]
    h_ref[...] = _rms(x, g_ref[...]).astype(BF16)
    for c in range(FFN_NCH):
        cs = slice(c * fc, (c + 1) * fc)
        g = jnp.dot(h_ref[...], wup_ref[:, cs], preferred_element_type=F32)
        u = jnp.dot(h_ref[...], wup_ref[:, D_FF + c * fc:D_FF + (c + 1) * fc],
                    preferred_element_type=F32)
        gbuf_ref[0:pfx, :] = carry_ref[:, cs]
        gbuf_ref[pfx:pfx + tm, :] = g
        carry_ref[:, cs] = g[tm - pfx:, :]
        gc = cb_ref[:, cs] + cw_ref[0:1, cs] * gbuf_ref[pfx - 2 * shift:pfx - 2 * shift + tm, :]
        gc = gc + cw_ref[1:2, cs] * gbuf_ref[pfx - shift:pfx - shift + tm, :]
        gc = gc + cw_ref[2:3, cs] * g
        act_ref[:, cs] = (_gelu(gc) * u).astype(BF16)
    y = jnp.dot(act_ref[...], wdn_ref[...], preferred_element_type=F32)
    o_ref[0] = x + y

    @pl.when(t == nt - 1)
    def _():
        tail_ref[0] = carry_ref[...]


def _conv_ffn(x, ctx, ln_g, w_up, conv_w, conv_b, w_down, *, tm, pfx, shift):
    B, T, D = x.shape
    fc = FFN_CHUNK
    wup = w_up.astype(BF16)
    body = functools.partial(_ffn_body, tm=tm, pfx=pfx, shift=shift)
    return pl.pallas_call(
        body,
        grid=(B, T // tm),
        in_specs=[pl.BlockSpec((1, tm, D), lambda b, t: (b, t, 0)),
                  _const_spec((1, D)),
                  _const_spec((D, 2 * D_FF)),
                  _const_spec((CONV_W, D_FF)),
                  _const_spec((1, D_FF)),
                  _const_spec((D_FF, D)),
                  pl.BlockSpec((1, pfx, D_FF), lambda b, t: (b, 0, 0))],
        out_specs=[pl.BlockSpec((1, tm, D), lambda b, t: (b, t, 0)),
                   pl.BlockSpec((1, pfx, D_FF), lambda b, t: (b, 0, 0))],
        out_shape=[jax.ShapeDtypeStruct((B, T, D), F32),
                   jax.ShapeDtypeStruct((B, pfx, D_FF), F32)],
        scratch_shapes=[pltpu.VMEM((tm, D), BF16),
                        pltpu.VMEM((pfx + tm, fc), F32),
                        pltpu.VMEM((pfx, D_FF), F32),
                        pltpu.VMEM((tm, D_FF), BF16)],
        compiler_params=_params(("arbitrary", "arbitrary")),
        name="conv_ffn_s%d" % shift,
    )(x, ln_g.reshape(1, D), wup, conv_w, conv_b.reshape(1, D_FF), w_down.astype(BF16), ctx)


def _qkv_body(x_ref, g_ref, w_ref, qg_ref, kg_ref, cos_ref, sin_ref, q_ref, k_ref, v_ref, h_ref,
              *, tm):
    h_ref[...] = _rms(x_ref[0], g_ref[...]).astype(BF16)
    lane = lax.broadcasted_iota(jnp.int32, (tm, LANES), 1)
    lo = lane < HEAD_DIM
    first = (lane & (HEAD_DIM - 1)) < (ROT_DIM // 2)
    cosv = cos_ref[...]
    sinv = sin_ref[...]
    nq = N_HEADS * HEAD_DIM // LANES
    nk = N_KV * HEAD_DIM // LANES
    blk = None
    for j in range(nq + nk):
        if j % 2 == 0:
            blk = jnp.dot(h_ref[...], w_ref[:, j * LANES:(j + 2) * LANES], preferred_element_type=F32)
        col = blk[:, (j % 2) * LANES:(j % 2 + 1) * LANES]
        sq = col * col
        s_lo = jnp.sum(jnp.where(lo, sq, 0.0), -1, keepdims=True)
        s_hi = jnp.sum(jnp.where(lo, 0.0, sq), -1, keepdims=True)
        ms = jnp.where(lo, s_lo, s_hi) * np.float32(1.0 / HEAD_DIM)
        gain = qg_ref[...] if j < nq else kg_ref[...]
        y = col * lax.rsqrt(ms + EPS) * gain
        fwd = pltpu.roll(y, ROT_DIM // 2, 1)
        bwd = pltpu.roll(y, LANES - ROT_DIM // 2, 1)
        out = y * cosv + jnp.where(first, bwd, fwd) * sinv
        if j < nq:
            q_ref[0, :, j * LANES:(j + 1) * LANES] = (out * np.float32(HEAD_DIM ** -0.5)).astype(BF16)
        else:
            k_ref[0, :, (j - nq) * LANES:(j - nq + 1) * LANES] = out
    v_ref[0] = jnp.dot(h_ref[...], w_ref[:, (N_HEADS + N_KV) * HEAD_DIM:], preferred_element_type=F32)


def _rope_tables(pos):
    half = ROT_DIM // 2
    inv = ROPE_THETA ** (-jnp.arange(half, dtype=F32) / half)
    ang = pos.astype(F32)[:, None] * inv[None, :]
    cos, sin = jnp.cos(ang), jnp.sin(ang)
    n = pos.shape[0]
    pad = HEAD_DIM - ROT_DIM
    c = jnp.concatenate([cos, cos, jnp.ones((n, pad), F32)], -1)
    s = jnp.concatenate([-sin, sin, jnp.zeros((n, pad), F32)], -1)
    return jnp.concatenate([c, c], -1), jnp.concatenate([s, s], -1)


def _swa_qkv(x, pos, ln_g, w_qkv, q_gain, k_gain, *, tm):
    B, T, D = x.shape
    cos, sin = _rope_tables(pos)
    kvw = N_KV * HEAD_DIM
    body = functools.partial(_qkv_body, tm=tm)
    return pl.pallas_call(
        body,
        grid=(B, T // tm),
        in_specs=[pl.BlockSpec((1, tm, D), lambda b, t: (b, t, 0)),
                  _const_spec((1, D)),
                  _const_spec((D, QKV_OUT)),
                  _const_spec((1, LANES)),
                  _const_spec((1, LANES)),
                  pl.BlockSpec((tm, LANES), lambda b, t: (t, 0)),
                  pl.BlockSpec((tm, LANES), lambda b, t: (t, 0))],
        out_specs=[pl.BlockSpec((1, tm, N_HEADS * HEAD_DIM), lambda b, t: (b, t, 0)),
                   pl.BlockSpec((1, tm, kvw), lambda b, t: (b, t, 0)),
                   pl.BlockSpec((1, tm, kvw), lambda b, t: (b, t, 0))],
        out_shape=[jax.ShapeDtypeStruct((B, T, N_HEADS * HEAD_DIM), BF16),
                   jax.ShapeDtypeStruct((B, T, kvw), F32),
                   jax.ShapeDtypeStruct((B, T, kvw), F32)],
        scratch_shapes=[pltpu.VMEM((tm, D), BF16)],
        compiler_params=_params(("arbitrary", "arbitrary")),
        name="swa_qkv",
    )(x, ln_g.reshape(1, D), w_qkv.astype(BF16), jnp.tile(q_gain, 2).reshape(1, LANES),
      jnp.tile(k_gain, 2).reshape(1, LANES), cos, sin)


def _both_halves(a):
    lane = lax.broadcasted_iota(jnp.int32, a.shape, a.ndim - 1)
    lo = lane < HEAD_DIM
    r = pltpu.roll(a, HEAD_DIM, a.ndim - 1)
    return jnp.where(lo, a, r), jnp.where(lo, r, a)


def _swa_prompt_body(sink_ref, x_ref, q_ref, kc_ref, kp_ref, vc_ref, vp_ref, wo_ref, o_ref, att_ref,
                     *, tq):
    t = pl.program_id(1)
    kall = jnp.concatenate([kp_ref[0], kc_ref[0]], axis=0)
    vall = jnp.concatenate([vp_ref[0], vc_ref[0]], axis=0)
    kb, vb = [], []
    for m in range(N_KV * HEAD_DIM // LANES):
        k0, k1 = _both_halves(kall[:, m * LANES:(m + 1) * LANES])
        v0, v1 = _both_halves(vall[:, m * LANES:(m + 1) * LANES])
        kb += [k0.astype(BF16), k1.astype(BF16)]
        vb += [v0.astype(BF16), v1.astype(BF16)]
    i = lax.broadcasted_iota(jnp.int32, (WINDOW, 2 * WINDOW), 0)
    s_idx = lax.broadcasted_iota(jnp.int32, (WINDOW, 2 * WINDOW), 1)
    band = (s_idx > i) & (s_idx <= i + WINDOW)
    first_ok = band & (s_idx >= jnp.where(t > 0, 0, WINDOW))
    band_bias = jnp.where(band, 0.0, -jnp.inf).astype(F32)
    first_bias = jnp.where(first_ok, 0.0, -jnp.inf).astype(F32)
    lane = lax.broadcasted_iota(jnp.int32, (WINDOW, LANES), 1)
    lo = lane < HEAD_DIM
    zero = jnp.zeros((WINDOW, LANES), BF16)
    for qb in range(tq // WINDOW):
        rs = slice(qb * WINDOW, (qb + 1) * WINDOW)
        ks = slice(qb * WINDOW, (qb + 2) * WINDOW)
        bias = first_bias if qb == 0 else band_bias
        for kv in range(N_KV):
            lhs = []
            for j in (2 * kv, 2 * kv + 1):
                qp = q_ref[0, rs, j * LANES:(j + 1) * LANES]
                lhs += [jnp.where(lo, qp, zero), jnp.where(lo, zero, qp)]
            lhs = jnp.concatenate(lhs, axis=0)
            s = lax.dot_general(lhs, kb[kv][ks], (((1,), (1,)), ((), ())),
                                preferred_element_type=F32)
            pvs = []
            for hh in range(GQA):
                sink = sink_ref[kv * GQA + hh]
                sh = s[hh * WINDOW:(hh + 1) * WINDOW] + bias
                mx = jnp.maximum(jnp.max(sh, -1, keepdims=True), sink)
                e = jnp.exp(sh - mx)
                r = 1.0 / (jnp.sum(e, -1, keepdims=True) + jnp.exp(sink - mx))
                pv = jnp.dot(e.astype(BF16), vb[kv][ks], preferred_element_type=F32)
                pvs.append(pv * r)
            o0 = jnp.where(lo, pvs[0], pvs[1])
            o1 = jnp.where(lo, pvs[2], pvs[3])
            att_ref[rs, 2 * kv * LANES:(2 * kv + 2) * LANES] = jnp.concatenate([o0, o1], 1).astype(BF16)
    o_ref[0] = x_ref[0] + jnp.dot(att_ref[...], wo_ref[...], preferred_element_type=F32)


def _swa_prompt_attn(x, q, k, v, sinks, w_o, *, tq):
    B, T, D = x.shape
    kvw = N_KV * HEAD_DIM
    qw = N_HEADS * HEAD_DIM
    r = tq // WINDOW
    body = functools.partial(_swa_prompt_body, tq=tq)
    cur = lambda b, t: (b, t, 0)
    prev = lambda b, t: (b, jnp.maximum(t * r - 1, 0), 0)
    return pl.pallas_call(
        body,
        grid=(B, T // tq),
        in_specs=[pl.BlockSpec(memory_space=pltpu.SMEM),
                  pl.BlockSpec((1, tq, D), cur),
                  pl.BlockSpec((1, tq, qw), cur),
                  pl.BlockSpec((1, tq, kvw), cur),
                  pl.BlockSpec((1, WINDOW, kvw), prev),
                  pl.BlockSpec((1, tq, kvw), cur),
                  pl.BlockSpec((1, WINDOW, kvw), prev),
                  _const_spec((qw, D))],
        out_specs=pl.BlockSpec((1, tq, D), cur),
        out_shape=jax.ShapeDtypeStruct((B, T, D), F32),
        scratch_shapes=[pltpu.VMEM((tq, qw), BF16)],
        compiler_params=_params(("arbitrary", "arbitrary")),
        name="swa_prompt_attn",
    )(sinks, x, q, k, k, v, v, w_o.astype(BF16))


def _swa_sample_body(sink_ref, q_ref, kn_ref, vn_ref, kc_ref, vc_ref, o_ref, ko_ref, vo_ref, *, sb, ts):
    pad = jnp.zeros((sb, WINDOW - ts, N_KV * HEAD_DIM), F32)
    kall = jnp.concatenate([kc_ref[...], kn_ref[...], pad], axis=1)
    vall = jnp.concatenate([vc_ref[...], vn_ref[...], pad], axis=1)
    ko_ref[...] = jnp.concatenate([kc_ref[:, ts:, :], kn_ref[...]], axis=1)
    vo_ref[...] = jnp.concatenate([vc_ref[:, ts:, :], vn_ref[...]], axis=1)
    nkeys = 2 * WINDOW
    rows = GQA * ts
    r_idx = lax.broadcasted_iota(jnp.int32, (sb, rows, nkeys), 1)
    s_idx = lax.broadcasted_iota(jnp.int32, (sb, rows, nkeys), 2)
    i = lax.rem(r_idx, ts)
    mask = (s_idx > i) & (s_idx <= i + WINDOW)
    hrow = lax.div(lax.broadcasted_iota(jnp.int32, (sb, rows, 1), 1), ts)
    lane = lax.broadcasted_iota(jnp.int32, (sb, ts, LANES), 2)
    lo = lane < HEAD_DIM
    qf = q_ref[...]
    for m in range(N_KV * HEAD_DIM // LANES):
        kpair = _both_halves(kall[:, :, m * LANES:(m + 1) * LANES])
        vpair = _both_halves(vall[:, :, m * LANES:(m + 1) * LANES])
        for half in range(2):
            kv = 2 * m + half
            kb = kpair[half].astype(BF16)
            vb = vpair[half].astype(BF16)
            lhs = []
            for j in (2 * kv, 2 * kv + 1):
                qp = qf[:, :, j * LANES:(j + 1) * LANES]
                lhs += [jnp.where(lo, qp, 0.0), jnp.where(lo, 0.0, qp)]
            lhs = jnp.concatenate(lhs, axis=1).astype(BF16)
            s = jnp.einsum('bqd,bkd->bqk', lhs, kb, preferred_element_type=F32)
            s = jnp.where(mask, s, -jnp.inf)
            sink = jnp.zeros((sb, rows, 1), F32)
            for hh in range(GQA):
                sink = jnp.where(hrow == hh, sink_ref[kv * GQA + hh], sink)
            mx = jnp.maximum(jnp.max(s, -1, keepdims=True), sink)
            e = jnp.exp(s - mx)
            r = 1.0 / (jnp.sum(e, -1, keepdims=True) + jnp.exp(sink - mx))
            pv = jnp.einsum('bqk,bkd->bqd', e.astype(BF16), vb, preferred_element_type=F32) * r
            o0 = jnp.where(lo, pv[:, 0:ts], pv[:, ts:2 * ts])
            o1 = jnp.where(lo, pv[:, 2 * ts:3 * ts], pv[:, 3 * ts:4 * ts])
            o_ref[:, :, 2 * kv * LANES:(2 * kv + 2) * LANES] = jnp.concatenate([o0, o1], 2)


def _swa_sample_attn(q, k_new, v_new, k_ctx, v_ctx, sinks, *, sb):
    N, ts, _ = q.shape
    kvw = N_KV * HEAD_DIM
    body = functools.partial(_swa_sample_body, sb=sb, ts=ts)
    blk = lambda n: (n, 0, 0)
    return pl.pallas_call(
        body,
        grid=(N // sb,),
        in_specs=[pl.BlockSpec(memory_space=pltpu.SMEM),
                  pl.BlockSpec((sb, ts, N_HEADS * HEAD_DIM), blk),
                  pl.BlockSpec((sb, ts, kvw), blk),
                  pl.BlockSpec((sb, ts, kvw), blk),
                  pl.BlockSpec((sb, WINDOW, kvw), blk),
                  pl.BlockSpec((sb, WINDOW, kvw), blk)],
        out_specs=[pl.BlockSpec((sb, ts, N_HEADS * HEAD_DIM), blk),
                   pl.BlockSpec((sb, WINDOW, kvw), blk),
                   pl.BlockSpec((sb, WINDOW, kvw), blk)],
        out_shape=[jax.ShapeDtypeStruct((N, ts, N_HEADS * HEAD_DIM), F32),
                   jax.ShapeDtypeStruct((N, WINDOW, kvw), F32),
                   jax.ShapeDtypeStruct((N, WINDOW, kvw), F32)],
        compiler_params=_params(("arbitrary",)),
        name="swa_sample_attn",
    )(sinks, q, k_new, v_new, k_ctx, v_ctx)


def _proj_res_body(x_ref, a_ref, w_ref, o_ref):
    o_ref[...] = x_ref[...] + jnp.dot(a_ref[...].astype(BF16), w_ref[...], preferred_element_type=F32)


def _proj_res(x, a, w, *, tm):
    R, D = x.shape
    K = a.shape[1]
    return pl.pallas_call(
        _proj_res_body,
        grid=(R // tm,),
        in_specs=[pl.BlockSpec((tm, D), lambda i: (i, 0)),
                  pl.BlockSpec((tm, K), lambda i: (i, 0)),
                  _const_spec((K, D))],
        out_specs=pl.BlockSpec((tm, D), lambda i: (i, 0)),
        out_shape=jax.ShapeDtypeStruct((R, D), F32),
        compiler_params=_params(("arbitrary",)),
        name="proj_res",
    )(x, a, w.astype(BF16))


def kernel(x_prompt, x_sample, cache_pool, cache_swa_k, cache_swa_v, cache_mem_k, cache_mem_v, cache_ffn_conv, mem_prompt, ln_mix, ln_mem, ln_memkv, ln_ffn, ab_w_in, ab_v_gain, ab_w_s, ab_b_s, ab_pool_w, ab_pool_scale, ab_w_out, c_w_qkv, c_q_gain, c_k_gain, c_sinks, c_w_o, m_w_q, m_w_kv, m_q_gain, m_k_gain, m_w_o, f_w_up, f_conv_w, f_conv_b, f_w_down):
    xp = x_prompt
    Bp, Tp, D = xp.shape
    Bs, Ts, _ = x_sample.shape
    depth = ln_mix.shape[0]
    tm = PROMPT_TILE
    kvw = N_KV * HEAD_DIM

    sblk = SAMPLE_BLOCK
    nblk = Bs // sblk

    def to_tm(a):
        n, c = a.shape[1], a.shape[2]
        return a.reshape(nblk, sblk, n, c).transpose(0, 2, 1, 3).reshape(nblk, n * sblk, c)

    def to_sm(a):
        n, c = a.shape[1] // sblk, a.shape[2]
        return a.reshape(nblk, n, sblk, c).transpose(0, 2, 1, 3).reshape(Bs, n, c)

    xs = x_sample
    pos_p = jnp.arange(Tp)
    pos_s = PAST_LEN + jnp.arange(Ts)

    mem_k, mem_v = _mem_kv(mem_prompt, ln_memkv, m_w_kv, m_k_gain)

    pool_p, pool_s, chunk_v_s = [], [], []
    swa_kp, swa_vp, swa_ks, swa_vs = [], [], [], []
    conv_p, conv_s = [], []
    ffn_pfx_p = SUBLANES
    ffn_pfx_s = (CONV_W - 1) * sblk
    for l in range(depth):
        j = l // 2
        if l % 2 == 0:
            pfx_p = 2 * SUBLANES
            zp = jnp.zeros((Bp, pfx_p, B_WIDTH), F32)
            xp, ptail = _mix_ab(xp, zp, ln_mix[l], ab_w_in[j], ab_v_gain[j], ab_w_s[j], ab_b_s[j],
                                ab_pool_w[j], ab_pool_scale[j], ab_w_out[j],
                                tm=tm, pfx=pfx_p, shift=1, pos0=0, sample=False)
            pool_p.append(ptail[:, pfx_p - POOL_CTX:])
            ctx_s = to_tm(cache_pool[j])
            xs_tm, p_s, v_s = _mix_ab(to_tm(xs), ctx_s, ln_mix[l], ab_w_in[j], ab_v_gain[j],
                                      ab_w_s[j], ab_b_s[j], ab_pool_w[j], ab_pool_scale[j],
                                      ab_w_out[j], tm=Ts * sblk, pfx=POOL_CTX * sblk, shift=sblk,
                                      pos0=PAST_LEN, sample=True)
            xs = to_sm(xs_tm)
            pool_s.append(jnp.concatenate([cache_pool[j], to_sm(p_s)], 1)[:, -POOL_CTX:])
            chunk_v_s.append(to_sm(v_s))
        else:
            q, k, v = _swa_qkv(xp, pos_p, ln_mix[l], c_w_qkv[j], c_q_gain[j], c_k_gain[j], tm=tm)
            xp = _swa_prompt_attn(xp, q, k, v, c_sinks[j], c_w_o[j], tq=tm)
            swa_kp.append(k[:, -WINDOW:].reshape(Bp, WINDOW, N_KV, HEAD_DIM))
            swa_vp.append(v[:, -WINDOW:].reshape(Bp, WINDOW, N_KV, HEAD_DIM))
            qs, ks_, vs_ = _swa_qkv(xs.reshape(1, Bs * Ts, D), jnp.tile(pos_s, Bs), ln_mix[l],
                                    c_w_qkv[j], c_q_gain[j], c_k_gain[j], tm=Bs * Ts)
            os_, kn, vn = _swa_sample_attn(qs.reshape(Bs, Ts, N_HEADS * HEAD_DIM).astype(F32),
                                           ks_.reshape(Bs, Ts, kvw), vs_.reshape(Bs, Ts, kvw),
                                           cache_swa_k[j].reshape(Bs, WINDOW, kvw),
                                           cache_swa_v[j].reshape(Bs, WINDOW, kvw), c_sinks[j], sb=16)
            xs = _proj_res(xs.reshape(Bs * Ts, D), os_.reshape(Bs * Ts, N_HEADS * HEAD_DIM),
                           c_w_o[j], tm=Bs * Ts).reshape(Bs, Ts, D)
            swa_ks.append(kn.reshape(Bs, WINDOW, N_KV, HEAD_DIM))
            swa_vs.append(vn.reshape(Bs, WINDOW, N_KV, HEAD_DIM))
        xp = _mem_attn_prompt(xp, mem_k, mem_v, l, ln_mem[l], m_w_q[l], m_q_gain[l], m_w_o[l],
                              tq=tm)
        xs = _mem_attn_sample(xs, cache_mem_k, cache_mem_v, l, ln_mem[l], m_w_q[l], m_q_gain[l],
                              m_w_o[l], nb=8)
        zc = jnp.zeros((Bp, ffn_pfx_p, D_FF), F32)
        xp, tail_p = _conv_ffn(xp, zc, ln_ffn[l], f_w_up[l], f_conv_w[l], f_conv_b[l], f_w_down[l],
                               tm=tm, pfx=ffn_pfx_p, shift=1)
        conv_p.append(tail_p[:, ffn_pfx_p - (CONV_W - 1):])
        xs_tm, tail_s = _conv_ffn(to_tm(xs), to_tm(cache_ffn_conv[l]), ln_ffn[l], f_w_up[l],
                                  f_conv_w[l], f_conv_b[l], f_w_down[l],
                                  tm=Ts * sblk, pfx=ffn_pfx_s, shift=sblk)
        xs = to_sm(xs_tm)
        conv_s.append(to_sm(tail_s))
    mk = mem_k.reshape(depth, Bp, N_MEM, MEM_HEADS, MEM_HD)
    mv = mem_v.reshape(depth, Bp, N_MEM, MEM_HEADS, MEM_HD)
    return (xp, xs,
            jnp.stack(pool_p), jnp.stack(pool_s), jnp.stack(chunk_v_s),
            jnp.stack(swa_kp), jnp.stack(swa_vp), jnp.stack(swa_ks), jnp.stack(swa_vs),
            mk, mv,
            jnp.stack(conv_p), jnp.stack(conv_s))
```

```python
import functools
import math

import jax
import jax.numpy as jnp
import numpy as np
from jax import lax
from jax.experimental import pallas as pl
from jax.experimental.pallas import tpu as pltpu

D_MODEL = 1024
PAST_LEN = 16384
CHUNK = 128
A_GROUPS = 4
A_GD = 128
A_WIDTH = A_GROUPS * A_GD
POOL_WINDOWS = (2, 4, 8, 16)
B_GD = 128
B_WIDTH = len(POOL_WINDOWS) * B_GD
POOL_CTX = max(POOL_WINDOWS) - 1
AB_IN = 2 * A_WIDTH + B_WIDTH
N_HEADS = 16
N_KV = 4
HEAD_DIM = 64
GQA = N_HEADS // N_KV
WINDOW = 128
ROT_DIM = HEAD_DIM // 4
ROPE_THETA = 500000.0
QKV_OUT = (N_HEADS + 2 * N_KV) * HEAD_DIM
N_MEM = 256
MEM_HEADS = 4
MEM_HD = 128
MEM_WIDTH = MEM_HEADS * MEM_HD
D_FF = 2816
CONV_W = 3
EPS = 1e-6

LANES = 128
SUBLANES = 8
VMEM_LIMIT = 56 * 1024 * 1024

F32 = jnp.float32
BF16 = jnp.bfloat16

AB_TILE = 512
MEM_TILE = 1024
FFN_TILE = 1024
QKV_TILE = 1024
SWA_TILE = 512
FFN_CHUNK = 256
FFN_NCH = D_FF // FFN_CHUNK
SAMPLE_BLOCK = 64


LOG2E = np.float32(math.log2(math.e))
Q_SCALE = np.float32(HEAD_DIM ** -0.5 * math.log2(math.e))


def _gelu(x):
    return 0.5 * x * (1.0 + lax.erf(x * np.float32(math.sqrt(0.5))))


def _rms(x, g):
    return x * lax.rsqrt(jnp.mean(x * x, -1, keepdims=True) + EPS) * g


def _const_spec(shape):
    zeros = (0,) * len(shape)
    return pl.BlockSpec(shape, lambda *_: zeros, pipeline_mode=pl.Buffered(1))


def _params(sem):
    return pltpu.CompilerParams(dimension_semantics=sem, vmem_limit_bytes=VMEM_LIMIT)


def _ab_body(x_ref, g_ref, win_ref, vgain_ref, wtril_ref, bsb_ref, ws_ref, bs_ref, pwbd_ref,
             pscale_ref, wout_ref, ctx_ref, *rest, tm, pfx, shift, pos0, sample):
    if sample:
        o_ref, ptail_ref, v_ref, pbuf_ref, carry_ref, cat_ref = rest
    else:
        o_ref, ptail_ref, pbuf_ref, carry_ref, cat_ref = rest
    t = pl.program_id(1)
    nt = pl.num_programs(1)

    @pl.when(t == 0)
    def _():
        carry_ref[...] = ctx_ref[0]

    x = x_ref[0]
    h = _rms(x, g_ref[...]).astype(BF16)
    proj = jnp.dot(h, win_ref[...], preferred_element_type=F32)
    u = _gelu(proj[:, :A_WIDTH])
    vpre = _gelu(proj[:, A_WIDTH:2 * A_WIDTH])
    p = proj[:, 2 * A_WIDTH:]
    mu = jnp.mean(vpre, -1, keepdims=True)
    xc = vpre - mu
    v = xc * lax.rsqrt(jnp.mean(xc * xc, -1, keepdims=True) + EPS) * vgain_ref[...]

    if sample:
        v_ref[0] = v
        nb = tm // shift
        for g in range(A_GROUPS):
            cs = slice(g * A_GD, (g + 1) * A_GD)
            for i in range(nb):
                acc = None
                for j in range(i + 1):
                    term = ws_ref[(g * nb + i) * nb + j] * v[j * shift:(j + 1) * shift, cs]
                    acc = term if acc is None else acc + term
                sg = acc + bs_ref[g * nb + i]
                a = u[i * shift:(i + 1) * shift, cs] * sg
                cat_ref[i * shift:(i + 1) * shift, cs] = a.astype(BF16)
    else:
        vb = v.astype(BF16)
        nc = tm // CHUNK
        for g in range(A_GROUPS):
            cs = slice(g * A_GD, (g + 1) * A_GD)
            rhs = jnp.concatenate([vb[c * CHUNK:(c + 1) * CHUNK, cs] for c in range(nc)], axis=1)
            sg = jnp.dot(wtril_ref[g], rhs, preferred_element_type=F32)
            for c in range(nc):
                rs = slice(c * CHUNK, (c + 1) * CHUNK)
                a = u[rs, cs] * (sg[:, c * A_GD:(c + 1) * A_GD] + bsb_ref[g])
                cat_ref[rs, cs] = a.astype(BF16)

    pbuf_ref[0:pfx, :] = carry_ref[...]
    pbuf_ref[pfx:pfx + tm, :] = p
    if tm >= pfx:
        carry_ref[...] = p[tm - pfx:, :]
    row = lax.broadcasted_iota(jnp.int32, (tm, B_GD), 0)
    pos = pos0 + (t * tm + row if shift == 1 else row // shift)
    pooled = []
    for gi, w in enumerate(POOL_WINDOWS):
        cs = slice(gi * B_GD, (gi + 1) * B_GD)
        acc = p[:, cs]
        for k in range(1, w):
            acc = acc + pbuf_ref[pfx - k * shift:pfx - k * shift + tm, cs]
        cnt = jnp.minimum(pos + 1, w).astype(F32)
        pooled.append(acc / cnt - p[:, cs])
    pooled = jnp.concatenate(pooled, axis=1).astype(BF16)
    b_out = jnp.dot(pooled, pwbd_ref[...], preferred_element_type=F32) * pscale_ref[...]
    cat_ref[:, A_WIDTH:] = b_out.astype(BF16)

    y = jnp.dot(cat_ref[...], wout_ref[...], preferred_element_type=F32)
    o_ref[0] = x + y

    if sample:
        ptail_ref[0] = p
    else:
        @pl.when(t == nt - 1)
        def _():
            ptail_ref[0] = carry_ref[...]


def _mix_ab(x, ctx, ln_g, w_in, v_gain, w_s, b_s, pool_w, pool_scale, w_out, *, tm, pfx, shift,
            pos0, sample):
    B, T, D = x.shape
    nb = T // shift if sample else SUBLANES
    tril = jnp.tril(jnp.ones((CHUNK, CHUNK), bool))
    wtril = jnp.where(tril, w_s, 0.0).astype(BF16)
    bsb = jnp.broadcast_to(b_s[:, :, None], (A_GROUPS, CHUNK, A_GD)).astype(F32)
    pwbd = jax.scipy.linalg.block_diag(*[pool_w[i] for i in range(len(POOL_WINDOWS))]).astype(BF16)
    ptail_rows = tm if sample else pfx
    out_shape = [jax.ShapeDtypeStruct((B, T, D), F32),
                 jax.ShapeDtypeStruct((B, ptail_rows, B_WIDTH), F32)]
    out_specs = [pl.BlockSpec((1, tm, D), lambda b, t: (b, t, 0)),
                 pl.BlockSpec((1, ptail_rows, B_WIDTH), lambda b, t: (b, 0, 0))]
    if sample:
        out_shape.append(jax.ShapeDtypeStruct((B, T, A_WIDTH), F32))
        out_specs.append(pl.BlockSpec((1, tm, A_WIDTH), lambda b, t: (b, t, 0)))
    smem = pl.BlockSpec(memory_space=pltpu.SMEM)
    body = functools.partial(_ab_body, tm=tm, pfx=pfx, shift=shift, pos0=pos0, sample=sample)
    return pl.pallas_call(
        body,
        grid=(B, T // tm),
        in_specs=[pl.BlockSpec((1, tm, D), lambda b, t: (b, t, 0)),
                  _const_spec((1, D)),
                  _const_spec((D, AB_IN)),
                  _const_spec((1, A_WIDTH)),
                  _const_spec((A_GROUPS, CHUNK, CHUNK)),
                  _const_spec((A_GROUPS, CHUNK, A_GD)),
                  smem, smem,
                  _const_spec((B_WIDTH, B_WIDTH)),
                  _const_spec((1, B_WIDTH)),
                  _const_spec((A_WIDTH + B_WIDTH, D)),
                  pl.BlockSpec((1, pfx, B_WIDTH), lambda b, t: (b, 0, 0))],
        out_specs=out_specs,
        out_shape=out_shape,
        scratch_shapes=[pltpu.VMEM((pfx + tm, B_WIDTH), F32),
                        pltpu.VMEM((pfx, B_WIDTH), F32),
                        pltpu.VMEM((tm, A_WIDTH + B_WIDTH), BF16)],
        compiler_params=_params(("arbitrary", "arbitrary")),
        name="mix_ab_sample" if sample else "mix_ab_prompt",
    )(x, ln_g.reshape(1, D), w_in.astype(BF16), v_gain.reshape(1, A_WIDTH), wtril, bsb,
      w_s[:, :nb, :nb].reshape(-1), b_s[:, :nb].reshape(-1), pwbd, pool_scale.reshape(1, B_WIDTH), w_out.astype(BF16), ctx)


def _memkv_body(mem_ref, g_ref, wkv_ref, kg_ref, k_ref, v_ref):
    h = _rms(mem_ref[0], g_ref[0]).astype(BF16)
    m = jnp.dot(h, wkv_ref[0], preferred_element_type=F32)
    ks = []
    for hd in range(MEM_HEADS):
        ks.append(_rms(m[:, hd * MEM_HD:(hd + 1) * MEM_HD], kg_ref[0]))
    k_ref[0, 0] = jnp.concatenate(ks, axis=1)
    v_ref[0, 0] = m[:, MEM_WIDTH:]


def _mem_kv(mem, ln_memkv, w_kv, k_gain):
    B = mem.shape[0]
    depth = w_kv.shape[0]
    shp = jax.ShapeDtypeStruct((depth, B, N_MEM, MEM_WIDTH), F32)
    return pl.pallas_call(
        _memkv_body,
        grid=(depth, B),
        in_specs=[pl.BlockSpec((1, N_MEM, D_MODEL), lambda l, b: (b, 0, 0)),
                  pl.BlockSpec((1, 1, D_MODEL), lambda l, b: (l, 0, 0)),
                  pl.BlockSpec((1, D_MODEL, 2 * MEM_WIDTH), lambda l, b: (l, 0, 0)),
                  pl.BlockSpec((1, 1, MEM_HD), lambda l, b: (l, 0, 0))],
        out_specs=[pl.BlockSpec((1, 1, N_MEM, MEM_WIDTH), lambda l, b: (l, b, 0, 0)),
                   pl.BlockSpec((1, 1, N_MEM, MEM_WIDTH), lambda l, b: (l, b, 0, 0))],
        out_shape=[shp, shp],
        compiler_params=_params(("arbitrary", "arbitrary")),
        name="mem_kv",
    )(mem, ln_memkv.reshape(depth, 1, D_MODEL), w_kv.astype(BF16), k_gain.reshape(depth, 1, MEM_HD))


def _mem_attn_prompt_body(x_ref, g_ref, wq_ref, qg_ref, k_ref, v_ref, wo_ref, o_ref):
    x = x_ref[0]
    h = _rms(x, g_ref[...]).astype(BF16)
    q = jnp.dot(h, wq_ref[...], preferred_element_type=F32)
    scale = np.float32(MEM_HD ** -0.5) * LOG2E
    outs = []
    for hd in range(MEM_HEADS):
        cs = slice(hd * MEM_HD, (hd + 1) * MEM_HD)
        qh = _rms(q[:, cs], qg_ref[...]).astype(BF16)
        kh = k_ref[0, 0, :, cs].astype(BF16)
        vh = v_ref[0, 0, :, cs].astype(BF16)
        s = lax.dot_general(qh, kh, (((1,), (1,)), ((), ())), preferred_element_type=F32) * scale
        m = jnp.max(s, -1, keepdims=True)
        e = jnp.exp2(s - m)
        r = 1.0 / jnp.sum(e, -1, keepdims=True)
        o = jnp.dot(e.astype(BF16), vh, preferred_element_type=F32) * r
        outs.append(o.astype(BF16))
    o = jnp.concatenate(outs, axis=1)
    o_ref[0] = x + jnp.dot(o, wo_ref[...], preferred_element_type=F32)


def _mem_attn_prompt(x, k, v, layer, ln_g, w_q, q_gain, w_o, *, tq):
    B, T, D = x.shape
    return pl.pallas_call(
        _mem_attn_prompt_body,
        grid=(B, T // tq),
        in_specs=[pl.BlockSpec((1, tq, D), lambda b, t: (b, t, 0)),
                  _const_spec((1, D)),
                  _const_spec((D, MEM_WIDTH)),
                  _const_spec((1, MEM_HD)),
                  pl.BlockSpec((1, 1, N_MEM, MEM_WIDTH), lambda b, t: (layer, b, 0, 0)),
                  pl.BlockSpec((1, 1, N_MEM, MEM_WIDTH), lambda b, t: (layer, b, 0, 0)),
                  _const_spec((MEM_WIDTH, D))],
        out_specs=pl.BlockSpec((1, tq, D), lambda b, t: (b, t, 0)),
        out_shape=jax.ShapeDtypeStruct((B, T, D), F32),
        compiler_params=_params(("arbitrary", "arbitrary")),
        name="mem_attn_prompt",
    )(x, ln_g.reshape(1, D), w_q.astype(BF16), q_gain.reshape(1, MEM_HD), k, v, w_o.astype(BF16))


def _mem_attn_sample_body(x_ref, g_ref, wq_ref, qg_ref, k_ref, v_ref, wo_ref, o_ref, *, nb, tq):
    nkeys = N_MEM * MEM_HEADS
    rows = MEM_HEADS * tq
    x = x_ref[...].reshape(nb * tq, D_MODEL)
    h = _rms(x, g_ref[...]).astype(BF16)
    q = jnp.dot(h, wq_ref[...], preferred_element_type=F32)
    qs = [_rms(q[:, hd * MEM_HD:(hd + 1) * MEM_HD], qg_ref[...]).reshape(nb, tq, MEM_HD)
          for hd in range(MEM_HEADS)]
    qa = jnp.concatenate(qs, axis=1).astype(BF16)
    s = jnp.einsum('bqd,bkd->bqk', qa, k_ref[0].astype(BF16), preferred_element_type=F32)
    row_head = lax.div(lax.broadcasted_iota(jnp.int32, (rows, nkeys), 0), tq)
    key_head = lax.broadcasted_iota(jnp.int32, (rows, nkeys), 1) & (MEM_HEADS - 1)
    bias = jnp.where(row_head == key_head, 0.0, -jnp.inf).astype(F32)
    s = s * (np.float32(MEM_HD ** -0.5) * LOG2E) + bias[None]
    m = jnp.max(s, -1, keepdims=True)
    e = jnp.exp2(s - m)
    r = 1.0 / jnp.sum(e, -1, keepdims=True)
    o = jnp.einsum('bqk,bkd->bqd', e.astype(BF16), v_ref[0].astype(BF16),
                   preferred_element_type=F32) * r
    o = jnp.concatenate([o[:, hd * tq:(hd + 1) * tq] for hd in range(MEM_HEADS)], axis=2)
    y = jnp.dot(o.reshape(nb * tq, MEM_WIDTH).astype(BF16), wo_ref[...], preferred_element_type=F32)
    o_ref[...] = (x + y).reshape(o_ref.shape)


def _mem_attn_sample(x, k_cache, v_cache, layer, ln_g, w_q, q_gain, w_o, *, nb):
    N, tq, D = x.shape
    depth = k_cache.shape[0]
    nkeys = N_MEM * MEM_HEADS
    k = k_cache.reshape(depth, N, nkeys, MEM_HD)
    v = v_cache.reshape(depth, N, nkeys, MEM_HD)
    body = functools.partial(_mem_attn_sample_body, nb=nb, tq=tq)
    return pl.pallas_call(
        body,
        grid=(N // nb,),
        in_specs=[pl.BlockSpec((nb, tq, D), lambda b: (b, 0, 0)),
                  _const_spec((1, D)),
                  _const_spec((D, MEM_WIDTH)),
                  _const_spec((1, MEM_HD)),
                  pl.BlockSpec((1, nb, nkeys, MEM_HD), lambda b: (layer, b, 0, 0)),
                  pl.BlockSpec((1, nb, nkeys, MEM_HD), lambda b: (layer, b, 0, 0)),
                  _const_spec((MEM_WIDTH, D))],
        out_specs=pl.BlockSpec((nb, tq, D), lambda b: (b, 0, 0)),
        out_shape=jax.ShapeDtypeStruct((N, tq, D), F32),
        compiler_params=_params(("arbitrary",)),
        name="mem_attn_sample",
    )(x, ln_g.reshape(1, D), w_q.astype(BF16), q_gain.reshape(1, MEM_HD), k, v, w_o.astype(BF16))


def _ffn_body(x_ref, g_ref, wup_ref, cw_ref, cb_ref, wdn_ref, ctx_ref, o_ref, tail_ref,
              h_ref, gbuf_ref, carry_ref, act_ref, *, tm, pfx, shift):
    t = pl.program_id(1)
    nt = pl.num_programs(1)
    fc = FFN_CHUNK

    @pl.when(t == 0)
    def _():
        carry_ref[...] = ctx_ref[0]

    x = x_ref[0]
    h_ref[...] = _rms(x, g_ref[...]).astype(BF16)
    for c in range(FFN_NCH):
        cs = slice(c * fc, (c + 1) * fc)
        g = jnp.dot(h_ref[...], wup_ref[:, cs], preferred_element_type=F32)
        u = jnp.dot(h_ref[...], wup_ref[:, D_FF + c * fc:D_FF + (c + 1) * fc],
                    preferred_element_type=F32)
        gbuf_ref[0:pfx, :] = carry_ref[:, cs]
        gbuf_ref[pfx:pfx + tm, :] = g
        carry_ref[:, cs] = g[tm - pfx:, :]
        gc = cb_ref[:, cs] + cw_ref[0:1, cs] * gbuf_ref[pfx - 2 * shift:pfx - 2 * shift + tm, :]
        gc = gc + cw_ref[1:2, cs] * gbuf_ref[pfx - shift:pfx - shift + tm, :]
        gc = gc + cw_ref[2:3, cs] * g
        act_ref[:, cs] = (_gelu(gc) * u).astype(BF16)
    y = jnp.dot(act_ref[...], wdn_ref[...], preferred_element_type=F32)
    o_ref[0] = x + y

    @pl.when(t == nt - 1)
    def _():
        tail_ref[0] = carry_ref[...]


def _conv_ffn(x, ctx, ln_g, w_up, conv_w, conv_b, w_down, *, tm, pfx, shift):
    B, T, D = x.shape
    fc = FFN_CHUNK
    wup = w_up.astype(BF16)
    body = functools.partial(_ffn_body, tm=tm, pfx=pfx, shift=shift)
    return pl.pallas_call(
        body,
        grid=(B, T // tm),
        in_specs=[pl.BlockSpec((1, tm, D), lambda b, t: (b, t, 0)),
                  _const_spec((1, D)),
                  _const_spec((D, 2 * D_FF)),
                  _const_spec((CONV_W, D_FF)),
                  _const_spec((1, D_FF)),
                  _const_spec((D_FF, D)),
                  pl.BlockSpec((1, pfx, D_FF), lambda b, t: (b, 0, 0))],
        out_specs=[pl.BlockSpec((1, tm, D), lambda b, t: (b, t, 0)),
                   pl.BlockSpec((1, pfx, D_FF), lambda b, t: (b, 0, 0))],
        out_shape=[jax.ShapeDtypeStruct((B, T, D), F32),
                   jax.ShapeDtypeStruct((B, pfx, D_FF), F32)],
        scratch_shapes=[pltpu.VMEM((tm, D), BF16),
                        pltpu.VMEM((pfx + tm, fc), F32),
                        pltpu.VMEM((pfx, D_FF), F32),
                        pltpu.VMEM((tm, D_FF), BF16)],
        compiler_params=_params(("arbitrary", "arbitrary")),
        name="conv_ffn_s%d" % shift,
    )(x, ln_g.reshape(1, D), wup, conv_w, conv_b.reshape(1, D_FF), w_down.astype(BF16), ctx)


def _qkv_body(x_ref, g_ref, w_ref, qg_ref, kg_ref, cos_ref, sin_ref, q_ref, k_ref, v_ref, h_ref,
              *, tm):
    h_ref[...] = _rms(x_ref[0], g_ref[...]).astype(BF16)
    lane = lax.broadcasted_iota(jnp.int32, (tm, LANES), 1)
    lo = lane < HEAD_DIM
    first = (lane & (HEAD_DIM - 1)) < (ROT_DIM // 2)
    cosv = cos_ref[...]
    sinv = sin_ref[...]
    nq = N_HEADS * HEAD_DIM // LANES
    nk = N_KV * HEAD_DIM // LANES
    blk = None
    for j in range(nq + nk):
        if j % 2 == 0:
            blk = jnp.dot(h_ref[...], w_ref[:, j * LANES:(j + 2) * LANES], preferred_element_type=F32)
        col = blk[:, (j % 2) * LANES:(j % 2 + 1) * LANES]
        sq = col * col
        s_lo = jnp.sum(jnp.where(lo, sq, 0.0), -1, keepdims=True)
        s_hi = jnp.sum(jnp.where(lo, 0.0, sq), -1, keepdims=True)
        ms = jnp.where(lo, s_lo, s_hi) * np.float32(1.0 / HEAD_DIM)
        gain = qg_ref[...] if j < nq else kg_ref[...]
        y = col * lax.rsqrt(ms + EPS) * gain
        fwd = pltpu.roll(y, ROT_DIM // 2, 1)
        bwd = pltpu.roll(y, LANES - ROT_DIM // 2, 1)
        out = y * cosv + jnp.where(first, bwd, fwd) * sinv
        if j < nq:
            q_ref[0, :, j * LANES:(j + 1) * LANES] = (out * Q_SCALE).astype(BF16)
        else:
            k_ref[0, :, (j - nq) * LANES:(j - nq + 1) * LANES] = out
    v_ref[0] = jnp.dot(h_ref[...], w_ref[:, (N_HEADS + N_KV) * HEAD_DIM:], preferred_element_type=F32)


def _rope_tables(pos):
    half = ROT_DIM // 2
    inv = ROPE_THETA ** (-np.arange(half, dtype=np.float64) / half)
    ang = np.asarray(pos, np.float64)[:, None] * inv[None, :]
    cos, sin = np.cos(ang), np.sin(ang)
    n = ang.shape[0]
    pad = HEAD_DIM - ROT_DIM
    c = np.concatenate([cos, cos, np.ones((n, pad))], -1)
    s = np.concatenate([-sin, sin, np.zeros((n, pad))], -1)
    return (jnp.asarray(np.concatenate([c, c], -1), F32), jnp.asarray(np.concatenate([s, s], -1), F32))


def _swa_qkv(x, pos, ln_g, w_qkv, q_gain, k_gain, *, tm):
    B, T, D = x.shape
    cos, sin = _rope_tables(pos)
    kvw = N_KV * HEAD_DIM
    body = functools.partial(_qkv_body, tm=tm)
    return pl.pallas_call(
        body,
        grid=(B, T // tm),
        in_specs=[pl.BlockSpec((1, tm, D), lambda b, t: (b, t, 0)),
                  _const_spec((1, D)),
                  _const_spec((D, QKV_OUT)),
                  _const_spec((1, LANES)),
                  _const_spec((1, LANES)),
                  pl.BlockSpec((tm, LANES), lambda b, t: (t, 0)),
                  pl.BlockSpec((tm, LANES), lambda b, t: (t, 0))],
        out_specs=[pl.BlockSpec((1, tm, N_HEADS * HEAD_DIM), lambda b, t: (b, t, 0)),
                   pl.BlockSpec((1, tm, kvw), lambda b, t: (b, t, 0)),
                   pl.BlockSpec((1, tm, kvw), lambda b, t: (b, t, 0))],
        out_shape=[jax.ShapeDtypeStruct((B, T, N_HEADS * HEAD_DIM), BF16),
                   jax.ShapeDtypeStruct((B, T, kvw), F32),
                   jax.ShapeDtypeStruct((B, T, kvw), F32)],
        scratch_shapes=[pltpu.VMEM((tm, D), BF16)],
        compiler_params=_params(("arbitrary", "arbitrary")),
        name="swa_qkv",
    )(x, ln_g.reshape(1, D), w_qkv.astype(BF16), jnp.tile(q_gain, 2).reshape(1, LANES),
      jnp.tile(k_gain, 2).reshape(1, LANES), cos, sin)


def _both_halves(a):
    lane = lax.broadcasted_iota(jnp.int32, a.shape, a.ndim - 1)
    lo = lane < HEAD_DIM
    r = pltpu.roll(a, HEAD_DIM, a.ndim - 1)
    return jnp.where(lo, a, r), jnp.where(lo, r, a)


def _swa_prompt_body(sink_ref, x_ref, q_ref, kc_ref, kp_ref, vc_ref, vp_ref, wo_ref, o_ref, att_ref,
                     *, tq):
    t = pl.program_id(1)
    kall = jnp.concatenate([kp_ref[0], kc_ref[0]], axis=0)
    vall = jnp.concatenate([vp_ref[0], vc_ref[0]], axis=0)
    kb, vb = [], []
    for m in range(N_KV * HEAD_DIM // LANES):
        k0, k1 = _both_halves(kall[:, m * LANES:(m + 1) * LANES])
        v0, v1 = _both_halves(vall[:, m * LANES:(m + 1) * LANES])
        kb += [k0.astype(BF16), k1.astype(BF16)]
        vb += [v0.astype(BF16), v1.astype(BF16)]
    row = lax.broadcasted_iota(jnp.int32, (WINDOW, WINDOW), 0)
    col = lax.broadcasted_iota(jnp.int32, (WINDOW, WINDOW), 1)
    upper = col > row
    lo = lax.broadcasted_iota(jnp.int32, (WINDOW, LANES), 1) < HEAD_DIM
    zero = jnp.zeros((WINDOW, LANES), BF16)
    no_prev = jnp.where(t > 0, 0.0, -jnp.inf).astype(F32)
    for qb in range(tq // WINDOW):
        rs = slice(qb * WINDOW, (qb + 1) * WINDOW)
        ks = slice(qb * WINDOW, (qb + 2) * WINDOW)
        for kv in range(N_KV):
            lhs = []
            for j in (2 * kv, 2 * kv + 1):
                qp = q_ref[0, rs, j * LANES:(j + 1) * LANES]
                lhs += [jnp.where(lo, qp, zero), jnp.where(lo, zero, qp)]
            lhs = jnp.concatenate(lhs, axis=0)
            s = lax.dot_general(lhs, kb[kv][ks], (((1,), (1,)), ((), ())),
                                preferred_element_type=F32)
            pvs = []
            for hh in range(GQA):
                sink = sink_ref[kv * GQA + hh] * LOG2E
                s_prev = s[hh * WINDOW:(hh + 1) * WINDOW, :WINDOW]
                s_cur = s[hh * WINDOW:(hh + 1) * WINDOW, WINDOW:]
                if qb == 0:
                    s_prev = s_prev + no_prev
                e = jnp.where(upper, s_prev, s_cur)
                mx = jnp.maximum(jnp.max(e, -1, keepdims=True), sink)
                p = jnp.exp2(e - mx)
                r = 1.0 / (jnp.sum(p, -1, keepdims=True) + jnp.exp2(sink - mx))
                p2 = jnp.concatenate([jnp.where(upper, p, 0.0), jnp.where(upper, 0.0, p)], axis=1)
                pvs.append(jnp.dot(p2.astype(BF16), vb[kv][ks], preferred_element_type=F32) * r)
            o0 = jnp.where(lo, pvs[0], pvs[1])
            o1 = jnp.where(lo, pvs[2], pvs[3])
            att_ref[rs, 2 * kv * LANES:(2 * kv + 2) * LANES] = jnp.concatenate([o0, o1], 1).astype(BF16)
    o_ref[0] = x_ref[0] + jnp.dot(att_ref[...], wo_ref[...], preferred_element_type=F32)


def _swa_prompt_attn(x, q, k, v, sinks, w_o, *, tq):
    B, T, D = x.shape
    kvw = N_KV * HEAD_DIM
    qw = N_HEADS * HEAD_DIM
    r = tq // WINDOW
    body = functools.partial(_swa_prompt_body, tq=tq)
    cur = lambda b, t: (b, t, 0)
    prev = lambda b, t: (b, jnp.maximum(t * r - 1, 0), 0)
    return pl.pallas_call(
        body,
        grid=(B, T // tq),
        in_specs=[pl.BlockSpec(memory_space=pltpu.SMEM),
                  pl.BlockSpec((1, tq, D), cur),
                  pl.BlockSpec((1, tq, qw), cur),
                  pl.BlockSpec((1, tq, kvw), cur),
                  pl.BlockSpec((1, WINDOW, kvw), prev),
                  pl.BlockSpec((1, tq, kvw), cur),
                  pl.BlockSpec((1, WINDOW, kvw), prev),
                  _const_spec((qw, D))],
        out_specs=pl.BlockSpec((1, tq, D), cur),
        out_shape=jax.ShapeDtypeStruct((B, T, D), F32),
        scratch_shapes=[pltpu.VMEM((tq, qw), BF16)],
        compiler_params=_params(("arbitrary", "arbitrary")),
        name="swa_prompt_attn",
    )(sinks, x, q, k, k, v, v, w_o.astype(BF16))


def _swa_sample_body(sink_ref, q_ref, kn_ref, vn_ref, kc_ref, vc_ref, o_ref, ko_ref, vo_ref, *, sb, ts):
    pad = jnp.zeros((sb, WINDOW - ts, N_KV * HEAD_DIM), F32)
    kall = jnp.concatenate([kc_ref[...], kn_ref[...], pad], axis=1)
    vall = jnp.concatenate([vc_ref[...], vn_ref[...], pad], axis=1)
    ko_ref[...] = jnp.concatenate([kc_ref[:, ts:, :], kn_ref[...]], axis=1)
    vo_ref[...] = jnp.concatenate([vc_ref[:, ts:, :], vn_ref[...]], axis=1)
    nkeys = 2 * WINDOW
    rows = GQA * ts
    r_idx = lax.broadcasted_iota(jnp.int32, (sb, rows, nkeys), 1)
    s_idx = lax.broadcasted_iota(jnp.int32, (sb, rows, nkeys), 2)
    i = lax.rem(r_idx, ts)
    mask = (s_idx > i) & (s_idx <= i + WINDOW)
    hrow = lax.div(lax.broadcasted_iota(jnp.int32, (sb, rows, 1), 1), ts)
    lane = lax.broadcasted_iota(jnp.int32, (sb, ts, LANES), 2)
    lo = lane < HEAD_DIM
    qf = q_ref[...]
    for m in range(N_KV * HEAD_DIM // LANES):
        kpair = _both_halves(kall[:, :, m * LANES:(m + 1) * LANES])
        vpair = _both_halves(vall[:, :, m * LANES:(m + 1) * LANES])
        for half in range(2):
            kv = 2 * m + half
            kb = kpair[half].astype(BF16)
            vb = vpair[half].astype(BF16)
            lhs = []
            for j in (2 * kv, 2 * kv + 1):
                qp = qf[:, :, j * LANES:(j + 1) * LANES]
                lhs += [jnp.where(lo, qp, 0.0), jnp.where(lo, 0.0, qp)]
            lhs = jnp.concatenate(lhs, axis=1).astype(BF16)
            s = jnp.einsum('bqd,bkd->bqk', lhs, kb, preferred_element_type=F32)
            s = jnp.where(mask, s, -jnp.inf)
            sink = jnp.zeros((sb, rows, 1), F32)
            for hh in range(GQA):
                sink = jnp.where(hrow == hh, sink_ref[kv * GQA + hh] * LOG2E, sink)
            mx = jnp.maximum(jnp.max(s, -1, keepdims=True), sink)
            e = jnp.exp2(s - mx)
            r = 1.0 / (jnp.sum(e, -1, keepdims=True) + jnp.exp2(sink - mx))
            pv = jnp.einsum('bqk,bkd->bqd', e.astype(BF16), vb, preferred_element_type=F32) * r
            o0 = jnp.where(lo, pv[:, 0:ts], pv[:, ts:2 * ts])
            o1 = jnp.where(lo, pv[:, 2 * ts:3 * ts], pv[:, 3 * ts:4 * ts])
            o_ref[:, :, 2 * kv * LANES:(2 * kv + 2) * LANES] = jnp.concatenate([o0, o1], 2)


def _swa_sample_attn(q, k_new, v_new, k_ctx, v_ctx, sinks, *, sb):
    N, ts, _ = q.shape
    kvw = N_KV * HEAD_DIM
    body = functools.partial(_swa_sample_body, sb=sb, ts=ts)
    blk = lambda n: (n, 0, 0)
    return pl.pallas_call(
        body,
        grid=(N // sb,),
        in_specs=[pl.BlockSpec(memory_space=pltpu.SMEM),
                  pl.BlockSpec((sb, ts, N_HEADS * HEAD_DIM), blk),
                  pl.BlockSpec((sb, ts, kvw), blk),
                  pl.BlockSpec((sb, ts, kvw), blk),
                  pl.BlockSpec((sb, WINDOW, kvw), blk),
                  pl.BlockSpec((sb, WINDOW, kvw), blk)],
        out_specs=[pl.BlockSpec((sb, ts, N_HEADS * HEAD_DIM), blk),
                   pl.BlockSpec((sb, WINDOW, kvw), blk),
                   pl.BlockSpec((sb, WINDOW, kvw), blk)],
        out_shape=[jax.ShapeDtypeStruct((N, ts, N_HEADS * HEAD_DIM), F32),
                   jax.ShapeDtypeStruct((N, WINDOW, kvw), F32),
                   jax.ShapeDtypeStruct((N, WINDOW, kvw), F32)],
        compiler_params=_params(("arbitrary",)),
        name="swa_sample_attn",
    )(sinks, q, k_new, v_new, k_ctx, v_ctx)


def _proj_res_body(x_ref, a_ref, w_ref, o_ref):
    o_ref[...] = x_ref[...] + jnp.dot(a_ref[...].astype(BF16), w_ref[...], preferred_element_type=F32)


def _proj_res(x, a, w, *, tm):
    R, D = x.shape
    K = a.shape[1]
    return pl.pallas_call(
        _proj_res_body,
        grid=(R // tm,),
        in_specs=[pl.BlockSpec((tm, D), lambda i: (i, 0)),
                  pl.BlockSpec((tm, K), lambda i: (i, 0)),
                  _const_spec((K, D))],
        out_specs=pl.BlockSpec((tm, D), lambda i: (i, 0)),
        out_shape=jax.ShapeDtypeStruct((R, D), F32),
        compiler_params=_params(("arbitrary",)),
        name="proj_res",
    )(x, a, w.astype(BF16))


def kernel(x_prompt, x_sample, cache_pool, cache_swa_k, cache_swa_v, cache_mem_k, cache_mem_v, cache_ffn_conv, mem_prompt, ln_mix, ln_mem, ln_memkv, ln_ffn, ab_w_in, ab_v_gain, ab_w_s, ab_b_s, ab_pool_w, ab_pool_scale, ab_w_out, c_w_qkv, c_q_gain, c_k_gain, c_sinks, c_w_o, m_w_q, m_w_kv, m_q_gain, m_k_gain, m_w_o, f_w_up, f_conv_w, f_conv_b, f_w_down):
    xp = x_prompt
    Bp, Tp, D = xp.shape
    Bs, Ts, _ = x_sample.shape
    depth = ln_mix.shape[0]
    kvw = N_KV * HEAD_DIM

    sblk = SAMPLE_BLOCK
    nblk = Bs // sblk

    def to_tm(a):
        n, c = a.shape[1], a.shape[2]
        return a.reshape(nblk, sblk, n, c).transpose(0, 2, 1, 3).reshape(nblk, n * sblk, c)

    def to_sm(a):
        n, c = a.shape[1] // sblk, a.shape[2]
        return a.reshape(nblk, n, sblk, c).transpose(0, 2, 1, 3).reshape(Bs, n, c)

    xs = x_sample
    pos_p = np.arange(Tp)
    pos_s = PAST_LEN + np.arange(Ts)

    mem_k, mem_v = _mem_kv(mem_prompt, ln_memkv, m_w_kv, m_k_gain)

    pool_p, pool_s, chunk_v_s = [], [], []
    swa_kp, swa_vp, swa_ks, swa_vs = [], [], [], []
    conv_p, conv_s = [], []
    ffn_pfx_p = SUBLANES
    ffn_pfx_s = (CONV_W - 1) * sblk
    for l in range(depth):
        j = l // 2
        if l % 2 == 0:
            pfx_p = 2 * SUBLANES
            zp = jnp.zeros((Bp, pfx_p, B_WIDTH), F32)
            xp, ptail = _mix_ab(xp, zp, ln_mix[l], ab_w_in[j], ab_v_gain[j], ab_w_s[j], ab_b_s[j],
                                ab_pool_w[j], ab_pool_scale[j], ab_w_out[j],
                                tm=AB_TILE, pfx=pfx_p, shift=1, pos0=0, sample=False)
            pool_p.append(ptail[:, pfx_p - POOL_CTX:])
            ctx_s = to_tm(cache_pool[j])
            xs_tm, p_s, v_s = _mix_ab(to_tm(xs), ctx_s, ln_mix[l], ab_w_in[j], ab_v_gain[j],
                                      ab_w_s[j], ab_b_s[j], ab_pool_w[j], ab_pool_scale[j],
                                      ab_w_out[j], tm=Ts * sblk, pfx=POOL_CTX * sblk, shift=sblk,
                                      pos0=PAST_LEN, sample=True)
            xs = to_sm(xs_tm)
            pool_s.append(jnp.concatenate([cache_pool[j], to_sm(p_s)], 1)[:, -POOL_CTX:])
            chunk_v_s.append(to_sm(v_s))
        else:
            q, k, v = _swa_qkv(xp, pos_p, ln_mix[l], c_w_qkv[j], c_q_gain[j], c_k_gain[j],
                               tm=QKV_TILE)
            xp = _swa_prompt_attn(xp, q, k, v, c_sinks[j], c_w_o[j], tq=SWA_TILE)
            swa_kp.append(k[:, -WINDOW:].reshape(Bp, WINDOW, N_KV, HEAD_DIM))
            swa_vp.append(v[:, -WINDOW:].reshape(Bp, WINDOW, N_KV, HEAD_DIM))
            qs, ks_, vs_ = _swa_qkv(xs.reshape(1, Bs * Ts, D), np.tile(pos_s, Bs), ln_mix[l],
                                    c_w_qkv[j], c_q_gain[j], c_k_gain[j], tm=Bs * Ts)
            os_, kn, vn = _swa_sample_attn(qs.reshape(Bs, Ts, N_HEADS * HEAD_DIM).astype(F32),
                                           ks_.reshape(Bs, Ts, kvw), vs_.reshape(Bs, Ts, kvw),
                                           cache_swa_k[j].reshape(Bs, WINDOW, kvw),
                                           cache_swa_v[j].reshape(Bs, WINDOW, kvw), c_sinks[j], sb=16)
            xs = _proj_res(xs.reshape(Bs * Ts, D), os_.reshape(Bs * Ts, N_HEADS * HEAD_DIM),
                           c_w_o[j], tm=Bs * Ts).reshape(Bs, Ts, D)
            swa_ks.append(kn.reshape(Bs, WINDOW, N_KV, HEAD_DIM))
            swa_vs.append(vn.reshape(Bs, WINDOW, N_KV, HEAD_DIM))
        xp = _mem_attn_prompt(xp, mem_k, mem_v, l, ln_mem[l], m_w_q[l], m_q_gain[l], m_w_o[l],
                              tq=MEM_TILE)
        xs = _mem_attn_sample(xs, cache_mem_k, cache_mem_v, l, ln_mem[l], m_w_q[l], m_q_gain[l],
                              m_w_o[l], nb=8)
        zc = jnp.zeros((Bp, ffn_pfx_p, D_FF), F32)
        xp, tail_p = _conv_ffn(xp, zc, ln_ffn[l], f_w_up[l], f_conv_w[l], f_conv_b[l], f_w_down[l],
                               tm=FFN_TILE, pfx=ffn_pfx_p, shift=1)
        conv_p.append(tail_p[:, ffn_pfx_p - (CONV_W - 1):])
        xs_tm, tail_s = _conv_ffn(to_tm(xs), to_tm(cache_ffn_conv[l]), ln_ffn[l], f_w_up[l],
                                  f_conv_w[l], f_conv_b[l], f_w_down[l],
                                  tm=Ts * sblk, pfx=ffn_pfx_s, shift=sblk)
        xs = to_sm(xs_tm)
        conv_s.append(to_sm(tail_s))
    mk = mem_k.reshape(depth, Bp, N_MEM, MEM_HEADS, MEM_HD)
    mv = mem_v.reshape(depth, Bp, N_MEM, MEM_HEADS, MEM_HD)
    return (xp, xs,
            jnp.stack(pool_p), jnp.stack(pool_s), jnp.stack(chunk_v_s),
            jnp.stack(swa_kp), jnp.stack(swa_vp), jnp.stack(swa_ks), jnp.stack(swa_vs),
            mk, mv,
            jnp.stack(conv_p), jnp.stack(conv_s))
```
